```python
import jax, jax.numpy as jnp
from jax import lax
import numpy as np

D_MODEL = 2048
BATCH = 4
SEQ = 4096
DEPTH = 4

N_MIXERS = 3
EXPAND = 2
D_INNER = EXPAND * D_MODEL
CHUNK = 128
SGU_HEADS = 8
SGU_HEAD_DIM = D_INNER // SGU_HEADS
POOL_WINDOWS = (2, 4, 8, 16)
POOL_GROUPS = len(POOL_WINDOWS)
POOL_GROUP_DIM = D_INNER // POOL_GROUPS
CONV_WIDTH = 31
LN_EPS = 1e-5
ALPHA = (2.0 * DEPTH) ** 0.25
BETA = (8.0 * DEPTH) ** -0.25
N_A = (DEPTH + 2) // 3
N_B = (DEPTH + 1) // 3
N_C = DEPTH // 3

kernel_name = 'hybrid_sgu_pool_conformer_deepnorm'


def layer_norm(x, g, b):
    xf = x.astype(jnp.float32)
    mu = jnp.mean(xf, axis=-1, keepdims=True)
    var = jnp.mean(jnp.square(xf - mu), axis=-1, keepdims=True)
    y = (xf - mu) * lax.rsqrt(var + LN_EPS) * g.astype(jnp.float32) + b.astype(jnp.float32)
    return y.astype(x.dtype)


def mixer_sgu(h, w_in, ln_g, ln_b, w_s, b_s, w_out):
    bsz, seq, _ = h.shape
    proj = h @ w_in
    u, v, z = jnp.split(proj, 3, axis=-1)
    u = jax.nn.gelu(u)
    v = layer_norm(jax.nn.gelu(v), ln_g, ln_b)
    vc = v.reshape(bsz, seq // CHUNK, CHUNK, SGU_HEADS, SGU_HEAD_DIM)
    causal = jnp.tril(jnp.ones((CHUNK, CHUNK), dtype=w_s.dtype))
    ws = w_s * causal[None]
    mixed = jnp.einsum('hts,bcshe->bcthe', ws, vc) + b_s.T[:, :, None]
    mixed = mixed.reshape(bsz, seq, D_INNER)
    y = u * mixed * jax.nn.silu(z)
    return y @ w_out


def mixer_pool(h, w_in, w_pool, scale, w_out):
    bsz, seq, _ = h.shape
    proj = h @ w_in
    v, z = jnp.split(proj, 2, axis=-1)
    vf = v.astype(jnp.float32).reshape(bsz, seq, POOL_GROUPS, POOL_GROUP_DIM)
    cs = jnp.cumsum(vf, axis=1)
    pos = jnp.arange(seq)
    outs = []
    for g, w in enumerate(POOL_WINDOWS):
        c = cs[:, :, g]
        lag = jnp.pad(c, ((0, 0), (w, 0), (0, 0)))[:, :seq]
        cnt = jnp.minimum(pos + 1, w).astype(jnp.float32)
        mean = (c - lag) / cnt[None, :, None]
        outs.append(mean - vf[:, :, g])
    p = jnp.stack(outs, axis=2).astype(v.dtype)
    p = jnp.einsum('bsgc,gcd->bsgd', p, w_pool).reshape(bsz, seq, D_INNER) * scale
    return (p * jax.nn.silu(z)) @ w_out


def mixer_conv(h, w_in, conv_w, conv_b, ln_g, ln_b, w_out):
    proj = h @ w_in
    a, gl, z = jnp.split(proj, 3, axis=-1)
    g = a * jax.nn.sigmoid(gl)
    c = lax.conv_general_dilated(
        g, conv_w[:, None, :], window_strides=(1,), padding=[(CONV_WIDTH - 1, 0)],
        dimension_numbers=('NWC', 'WIO', 'NWC'), feature_group_count=D_INNER) + conv_b
    s = jax.nn.silu(layer_norm(c, ln_g, ln_b))
    return (s * jax.nn.silu(z)) @ w_out


def setup_inputs(seed: int = 0) -> dict:
    key = jax.random.key(seed)
    ks = jax.random.split(key, 24)
    f32 = jnp.float32
    nrm = lambda k, shape: jax.random.normal(k, shape, dtype=f32)
    E, D = D_INNER, D_MODEL
    out_scale = BETA * E ** -0.5
    return {
        'x': nrm(ks[0], (BATCH, SEQ, D)),
        'a_w_in': nrm(ks[1], (N_A, D, 3 * E)) * D ** -0.5,
        'a_ln_g': 1.0 + 0.02 * nrm(ks[2], (N_A, E)),
        'a_ln_b': 0.02 * nrm(ks[3], (N_A, E)),
        'a_w_s': nrm(ks[4], (N_A, SGU_HEADS, CHUNK, CHUNK)) * CHUNK ** -0.5,
        'a_b_s': 1.0 + 0.02 * nrm(ks[5], (N_A, SGU_HEADS, CHUNK)),
        'a_w_out': nrm(ks[6], (N_A, E, D)) * out_scale,
        'b_w_in': nrm(ks[7], (N_B, D, 2 * E)) * D ** -0.5,
        'b_w_pool': nrm(ks[8], (N_B, POOL_GROUPS, POOL_GROUP_DIM, POOL_GROUP_DIM)) * POOL_GROUP_DIM ** -0.5,
        'b_scale': 1.0 + 0.1 * nrm(ks[9], (N_B, E)),
        'b_w_out': nrm(ks[10], (N_B, E, D)) * out_scale,
        'c_w_in': nrm(ks[11], (N_C, D, 3 * E)) * D ** -0.5,
        'c_conv_w': nrm(ks[12], (N_C, CONV_WIDTH, E)) * CONV_WIDTH ** -0.5,
        'c_conv_b': 0.02 * nrm(ks[13], (N_C, E)),
        'c_ln_g': 1.0 + 0.02 * nrm(ks[14], (N_C, E)),
        'c_ln_b': 0.02 * nrm(ks[15], (N_C, E)),
        'c_w_out': nrm(ks[16], (N_C, E, D)) * out_scale,
        'post_ln_g': 1.0 + 0.02 * nrm(ks[17], (DEPTH, D)),
        'post_ln_b': 0.02 * nrm(ks[18], (DEPTH, D)),
    }


def reference(x, a_w_in, a_ln_g, a_ln_b, a_w_s, a_b_s, a_w_out,
              b_w_in, b_w_pool, b_scale, b_w_out,
              c_w_in, c_conv_w, c_conv_b, c_ln_g, c_ln_b, c_w_out,
              post_ln_g, post_ln_b):
    h = x
    for i in range(DEPTH):
        kind, j = i % N_MIXERS, i // N_MIXERS
        if kind == 0:
            y = mixer_sgu(h, a_w_in[j], a_ln_g[j], a_ln_b[j], a_w_s[j], a_b_s[j], a_w_out[j])
        elif kind == 1:
            y = mixer_pool(h, b_w_in[j], b_w_pool[j], b_scale[j], b_w_out[j])
        else:
            y = mixer_conv(h, c_w_in[j], c_conv_w[j], c_conv_b[j], c_ln_g[j], c_ln_b[j], c_w_out[j])
        h = layer_norm(ALPHA * h + y, post_ln_g[i], post_ln_b[i])
    return h
```

```python
import functools
import math

import jax
import jax.numpy as jnp
from jax import lax
from jax.experimental import pallas as pl
from jax.experimental.pallas import tpu as pltpu

D_MODEL = 2048
BATCH = 4
SEQ = 4096
DEPTH = 4
N_MIXERS = 3
D_INNER = 2 * D_MODEL
CHUNK = 128
SGU_HEADS = 8
SGU_HEAD_DIM = D_INNER // SGU_HEADS
POOL_WINDOWS = (2, 4, 8, 16)
POOL_GROUP_DIM = D_INNER // len(POOL_WINDOWS)
CONV_WIDTH = 31
LN_EPS = 1e-5
ALPHA = (2.0 * DEPTH) ** 0.25

N_TOKENS = BATCH * SEQ
POOL_HALO = 16
CONV_HALO = 32
CONV_LANES = 512

PROJ_TM = 1024
PROJ_TN = 1024
OUT_TM = 256
VMEM_LIMIT = 56 * 1024 * 1024

_F32 = jnp.float32
_BF16 = jnp.bfloat16
_GELU_C = math.sqrt(2.0 / math.pi)


def _gelu(x):
    return 0.5 * x * (1.0 + jnp.tanh(_GELU_C * (x + 0.044715 * (x * x * x))))


def _sigmoid(x):
    return 0.5 * (1.0 + jnp.tanh(0.5 * x))


def _silu(x):
    return x * _sigmoid(x)


def _glu(a, g):
    return a * _sigmoid(g)


def _layer_norm(x, g, b):
    mu = jnp.mean(x, axis=-1, keepdims=True)
    xc = x - mu
    var = jnp.mean(xc * xc, axis=-1, keepdims=True)
    return xc * lax.rsqrt(var + LN_EPS) * g + b


def _proj_kernel(x_ref, *refs, n_w, epilogue):
    o_ref = refs[n_w]
    x = x_ref[...]
    accs = [jnp.dot(x, w_ref[...], preferred_element_type=_F32) for w_ref in refs[:n_w]]
    o_ref[...] = epilogue(*accs).astype(o_ref.dtype)


def _proj(x, w, col_tile_offsets, n_col_tiles, epilogue, name):
    n, k = x.shape
    tm, tn = PROJ_TM, PROJ_TN
    in_specs = [pl.BlockSpec((tm, k), lambda j, i: (i, 0))]
    for off in col_tile_offsets:
        in_specs.append(pl.BlockSpec((k, tn), lambda j, i, off=off: (0, j + off)))
    return pl.pallas_call(
        functools.partial(_proj_kernel, n_w=len(col_tile_offsets), epilogue=epilogue),
        grid=(n_col_tiles, n // tm),
        in_specs=in_specs,
        out_specs=pl.BlockSpec((tm, tn), lambda j, i: (i, j)),
        out_shape=jax.ShapeDtypeStruct((n, n_col_tiles * tn), _BF16),
        compiler_params=pltpu.CompilerParams(
            dimension_semantics=("arbitrary", "arbitrary"), vmem_limit_bytes=VMEM_LIMIT),
        name=name,
    )(x, *([w] * len(col_tile_offsets)))


def _out_proj_tail(y_scr, h_ref, wout_ref, plg_ref, plb_ref, of_ref, ob_ref):
    acc = jnp.dot(y_scr[...], wout_ref[...], preferred_element_type=_F32)
    o = _layer_norm(ALPHA * h_ref[...] + acc, plg_ref[...], plb_ref[...])
    of_ref[...] = o
    ob_ref[...] = o.astype(_BF16)


def _const_spec(shape):
    return pl.BlockSpec(shape, lambda i: (0,) * len(shape), pipeline_mode=pl.Buffered(1))


def _out_call(kernel, name, tile_inputs, const_inputs, scratch_shapes):
    tm = OUT_TM
    in_specs = [pl.BlockSpec(bs, im) for _, bs, im in tile_inputs]
    in_specs += [_const_spec(a.shape) for a in const_inputs]
    out_spec = pl.BlockSpec((tm, D_MODEL), lambda i: (i, 0))
    return pl.pallas_call(
        kernel,
        grid=(N_TOKENS // tm,),
        in_specs=in_specs,
        out_specs=[out_spec, out_spec],
        out_shape=[jax.ShapeDtypeStruct((N_TOKENS, D_MODEL), _F32),
                   jax.ShapeDtypeStruct((N_TOKENS, D_MODEL), _BF16)],
        scratch_shapes=scratch_shapes,
        compiler_params=pltpu.CompilerParams(
            dimension_semantics=("arbitrary",), vmem_limit_bytes=VMEM_LIMIT),
        name=name,
    )(*[a for a, _, _ in tile_inputs], *const_inputs)


def _halo_spec(rows):
    per_tile = OUT_TM // rows
    return (rows, D_INNER), (lambda i: (jnp.maximum(i * per_tile - 1, 0), 0))


def _sgu_out_kernel(gu_ref, gv_ref, sz_ref, h_ref, lng_ref, lnb_ref, ws_ref, bs_ref,
                    wout_ref, plg_ref, plb_ref, of_ref, ob_ref, vn_scr, y_scr):
    tm = gu_ref.shape[0]
    vn_scr[...] = _layer_norm(gv_ref[...].astype(_F32), lng_ref[...], lnb_ref[...]).astype(_BF16)
    row = lax.broadcasted_iota(jnp.int32, (CHUNK, CHUNK), 0)
    col = lax.broadcasted_iota(jnp.int32, (CHUNK, CHUNK), 1)
    causal = row >= col
    for hd in range(SGU_HEADS):
        w = jnp.where(causal, ws_ref[hd], 0.0).astype(_BF16)
        bias = bs_ref[:, hd:hd + 1]
        cs = slice(hd * SGU_HEAD_DIM, (hd + 1) * SGU_HEAD_DIM)
        for c in range(tm // CHUNK):
            rs = slice(c * CHUNK, (c + 1) * CHUNK)
            mixed = jnp.dot(w, vn_scr[rs, cs], preferred_element_type=_F32) + bias
            y = gu_ref[rs, cs].astype(_F32) * mixed * sz_ref[rs, cs].astype(_F32)
            y_scr[rs, cs] = y.astype(_BF16)
    _out_proj_tail(y_scr, h_ref, wout_ref, plg_ref, plb_ref, of_ref, ob_ref)


def _sgu_out(guv, sz, h_f, ln_g, ln_b, w_s, b_s_t, w_out, pl_g, pl_b):
    tm = OUT_TM
    tile = (tm, D_INNER)
    return _out_call(
        _sgu_out_kernel, "sgu_out",
        [(guv, tile, lambda i: (i, 0)), (guv, tile, lambda i: (i, 1)), (sz, tile, lambda i: (i, 0)),
         (h_f, (tm, D_MODEL), lambda i: (i, 0))],
        [ln_g, ln_b, w_s, b_s_t, w_out, pl_g, pl_b],
        [pltpu.VMEM(tile, _BF16), pltpu.VMEM(tile, _BF16)])


def _pool_out_kernel(v_ref, vh_ref, sz_ref, h_ref, wpool_ref, scale_ref,
                     wout_ref, plg_ref, plb_ref, of_ref, ob_ref, y_scr):
    tm = v_ref.shape[0]
    pos0 = (pl.program_id(0) * tm) % SEQ
    seq_start = pos0 == 0
    gd = POOL_GROUP_DIM
    pos = pos0 + lax.broadcasted_iota(jnp.int32, (tm, gd), 0)
    for g, win in enumerate(POOL_WINDOWS):
        cs = slice(g * gd, (g + 1) * gd)
        halo = jnp.where(seq_start, 0.0, vh_ref[:, cs].astype(_F32))
        ext = jnp.concatenate([halo, v_ref[:, cs].astype(_F32)], axis=0)
        s, k = ext, 1
        while k < win:
            s = s + pltpu.roll(s, k, axis=0)
            k *= 2
        cnt = jnp.minimum(pos + 1, win).astype(_F32)
        p = s[POOL_HALO:] / cnt - ext[POOL_HALO:]
        q = jnp.dot(p.astype(_BF16), wpool_ref[g], preferred_element_type=_F32) * scale_ref[:, cs]
        y_scr[:, cs] = (q * sz_ref[:, cs].astype(_F32)).astype(_BF16)
    _out_proj_tail(y_scr, h_ref, wout_ref, plg_ref, plb_ref, of_ref, ob_ref)


def _pool_out(v, sz, h_f, w_pool, scale, w_out, pl_g, pl_b):
    tm = OUT_TM
    tile = (tm, D_INNER)
    halo_shape, halo_map = _halo_spec(POOL_HALO)
    return _out_call(
        _pool_out_kernel, "pool_out",
        [(v, tile, lambda i: (i, 0)), (v, halo_shape, halo_map), (sz, tile, lambda i: (i, 0)),
         (h_f, (tm, D_MODEL), lambda i: (i, 0))],
        [w_pool, scale, w_out, pl_g, pl_b],
        [pltpu.VMEM(tile, _BF16)])


def _conv_out_kernel(g_ref, gh_ref, sz_ref, h_ref, cw_ref, cb_ref, lng_ref, lnb_ref,
                     wout_ref, plg_ref, plb_ref, of_ref, ob_ref, c_scr, y_scr):
    tm = g_ref.shape[0]
    seq_start = (pl.program_id(0) * tm) % SEQ == 0
    for blk in range(D_INNER // CONV_LANES):
        cs = slice(blk * CONV_LANES, (blk + 1) * CONV_LANES)
        halo = jnp.where(seq_start, 0.0, gh_ref[:, cs].astype(_F32))
        ext = jnp.concatenate([halo, g_ref[:, cs].astype(_F32)], axis=0)
        rolled = [ext] + [pltpu.roll(ext, r, axis=0) for r in range(1, 8)]
        acc = jnp.broadcast_to(cb_ref[:, cs], (tm, CONV_LANES))
        for d in range(CONV_WIDTH):
            q, r = divmod(d, 8)
            lo = CONV_HALO - 8 * q
            tap = CONV_WIDTH - 1 - d
            acc = acc + rolled[r][lo:lo + tm] * cw_ref[tap:tap + 1, cs]
        c_scr[:, cs] = acc
    s = _silu(_layer_norm(c_scr[...], lng_ref[...], lnb_ref[...]))
    y_scr[...] = (s * sz_ref[...].astype(_F32)).astype(_BF16)
    _out_proj_tail(y_scr, h_ref, wout_ref, plg_ref, plb_ref, of_ref, ob_ref)


def _conv_out(g, sz, h_f, conv_w, conv_b, ln_g, ln_b, w_out, pl_g, pl_b):
    tm = OUT_TM
    tile = (tm, D_INNER)
    halo_shape, halo_map = _halo_spec(CONV_HALO)
    return _out_call(
        _conv_out_kernel, "conv_out",
        [(g, tile, lambda i: (i, 0)), (g, halo_shape, halo_map), (sz, tile, lambda i: (i, 0)),
         (h_f, (tm, D_MODEL), lambda i: (i, 0))],
        [conv_w, conv_b, ln_g, ln_b, w_out, pl_g, pl_b],
        [pltpu.VMEM(tile, _F32), pltpu.VMEM(tile, _BF16)])


def _row(v):
    return v.reshape(1, -1).astype(_F32)


def kernel(x, a_w_in, a_ln_g, a_ln_b, a_w_s, a_b_s, a_w_out, b_w_in, b_w_pool, b_scale, b_w_out,
           c_w_in, c_conv_w, c_conv_b, c_ln_g, c_ln_b, c_w_out, post_ln_g, post_ln_b):
    e_tiles = D_INNER // PROJ_TN
    h_f = x.reshape(N_TOKENS, D_MODEL)
    h_b = h_f.astype(_BF16)
    for i in range(DEPTH):
        kind, j = i % N_MIXERS, i // N_MIXERS
        pl_g, pl_b = _row(post_ln_g[i]), _row(post_ln_b[i])
        if kind == 0:
            w_in, w_out = a_w_in[j].astype(_BF16), a_w_out[j].astype(_BF16)
            guv = _proj(h_b, w_in, [0], 2 * e_tiles, _gelu, "sgu_in_uv")
            sz = _proj(h_b, w_in, [2 * e_tiles], e_tiles, _silu, "sgu_in_z")
            h_f, h_b = _sgu_out(guv, sz, h_f, _row(a_ln_g[j]), _row(a_ln_b[j]), a_w_s[j],
                                a_b_s[j].T, w_out, pl_g, pl_b)
        elif kind == 1:
            w_in, w_out = b_w_in[j].astype(_BF16), b_w_out[j].astype(_BF16)
            v = _proj(h_b, w_in, [0], e_tiles, lambda a: a, "pool_in_v")
            sz = _proj(h_b, w_in, [e_tiles], e_tiles, _silu, "pool_in_z")
            h_f, h_b = _pool_out(v, sz, h_f, b_w_pool[j].astype(_BF16), _row(b_scale[j]),
                                 w_out, pl_g, pl_b)
        else:
            w_in, w_out = c_w_in[j].astype(_BF16), c_w_out[j].astype(_BF16)
            g = _proj(h_b, w_in, [0, e_tiles], e_tiles, _glu, "conv_in_glu")
            sz = _proj(h_b, w_in, [2 * e_tiles], e_tiles, _silu, "conv_in_z")
            h_f, h_b = _conv_out(g, sz, h_f, c_conv_w[j], _row(c_conv_b[j]), _row(c_ln_g[j]),
                                 _row(c_ln_b[j]), w_out, pl_g, pl_b)
    return h_f.reshape(BATCH, SEQ, D_MODEL)
```

```python
import functools
import math

import jax
import jax.numpy as jnp
from jax import lax
from jax.experimental import pallas as pl
from jax.experimental.pallas import tpu as pltpu

D_MODEL = 2048
BATCH = 4
SEQ = 4096
DEPTH = 4
N_MIXERS = 3
D_INNER = 2 * D_MODEL
CHUNK = 128
SGU_HEADS = 8
SGU_HEAD_DIM = D_INNER // SGU_HEADS
POOL_WINDOWS = (2, 4, 8, 16)
POOL_GROUP_DIM = D_INNER // len(POOL_WINDOWS)
CONV_WIDTH = 31
LN_EPS = 1e-5
ALPHA = (2.0 * DEPTH) ** 0.25

N_TOKENS = BATCH * SEQ
POOL_HALO = 16
CONV_HALO = 32
LANES = 128
CONV_ROW_STRIDE = 4

PROJ_TM = 1024
PROJ_TN = 1024
OUT_TM = 256
VMEM_LIMIT = 56 * 1024 * 1024

_F32 = jnp.float32
_BF16 = jnp.bfloat16
_GELU_C = math.sqrt(2.0 / math.pi)


def _gelu(x):
    return 0.5 * x * (1.0 + jnp.tanh(_GELU_C * (x + 0.044715 * (x * x * x))))


def _sigmoid(x):
    return 0.5 * (1.0 + jnp.tanh(0.5 * x))


def _silu(x):
    return x * _sigmoid(x)


def _glu(a, g):
    return a * _sigmoid(g)


def _layer_norm(x, g, b):
    mu = jnp.mean(x, axis=-1, keepdims=True)
    xc = x - mu
    var = jnp.mean(xc * xc, axis=-1, keepdims=True)
    return xc * lax.rsqrt(var + LN_EPS) * g + b


def _proj_kernel(x_ref, *refs, n_w, epilogue):
    w_refs, o_ref, wb_refs = refs[:n_w], refs[n_w], refs[n_w + 1:]

    @pl.when(pl.program_id(1) == 0)
    def _():
        for w_ref, wb_ref in zip(w_refs, wb_refs):
            wb_ref[...] = w_ref[...].astype(_BF16)

    x = x_ref[...]
    accs = [jnp.dot(x, wb_ref[...], preferred_element_type=_F32) for wb_ref in wb_refs]
    o_ref[...] = epilogue(*accs).astype(o_ref.dtype)


def _proj(x, w, layer, col_tile_offsets, n_col_tiles, epilogue, name):
    n, k = x.shape
    tm, tn = PROJ_TM, PROJ_TN
    n_w = len(col_tile_offsets)
    w_mode = pl.Buffered(1) if n_w > 1 else None
    in_specs = [pl.BlockSpec((tm, k), lambda j, i: (i, 0))]
    for off in col_tile_offsets:
        in_specs.append(pl.BlockSpec((None, k, tn), lambda j, i, off=off: (layer, 0, j + off),
                                     pipeline_mode=w_mode))
    return pl.pallas_call(
        functools.partial(_proj_kernel, n_w=n_w, epilogue=epilogue),
        grid=(n_col_tiles, n // tm),
        in_specs=in_specs,
        out_specs=pl.BlockSpec((tm, tn), lambda j, i: (i, j)),
        out_shape=jax.ShapeDtypeStruct((n, n_col_tiles * tn), _BF16),
        scratch_shapes=[pltpu.VMEM((k, tn), _BF16)] * n_w,
        compiler_params=pltpu.CompilerParams(
            dimension_semantics=("arbitrary", "arbitrary"), vmem_limit_bytes=VMEM_LIMIT),
        name=name,
    )(x, *([w] * n_w))


def _out_proj_tail(y_scr, h_ref, wout_ref, plg_ref, plb_ref, of_ref, ob_ref):
    acc = jnp.dot(y_scr[...], wout_ref[...], preferred_element_type=_F32)
    o = _layer_norm(ALPHA * h_ref[...] + acc, plg_ref[...], plb_ref[...])
    of_ref[...] = o
    ob_ref[...] = o.astype(_BF16)


def _const_spec(shape):
    return pl.BlockSpec(shape, lambda i: (0,) * len(shape), pipeline_mode=pl.Buffered(1))


def _out_call(kernel, name, tile_inputs, const_inputs, scratch_shapes):
    tm = OUT_TM
    in_specs = [pl.BlockSpec(bs, im) for _, bs, im in tile_inputs]
    in_specs += [_const_spec(a.shape) for a in const_inputs]
    out_spec = pl.BlockSpec((tm, D_MODEL), lambda i: (i, 0))
    return pl.pallas_call(
        kernel,
        grid=(N_TOKENS // tm,),
        in_specs=in_specs,
        out_specs=[out_spec, out_spec],
        out_shape=[jax.ShapeDtypeStruct((N_TOKENS, D_MODEL), _F32),
                   jax.ShapeDtypeStruct((N_TOKENS, D_MODEL), _BF16)],
        scratch_shapes=scratch_shapes,
        compiler_params=pltpu.CompilerParams(
            dimension_semantics=("arbitrary",), vmem_limit_bytes=VMEM_LIMIT),
        name=name,
    )(*[a for a, _, _ in tile_inputs], *const_inputs)


def _halo_spec(rows):
    per_tile = OUT_TM // rows
    return (rows, D_INNER), (lambda i: (jnp.maximum(i * per_tile - 1, 0), 0))


def _sgu_out_kernel(gu_ref, gv_ref, sz_ref, h_ref, lng_ref, lnb_ref, ws_ref, bs_ref,
                    wout_ref, plg_ref, plb_ref, of_ref, ob_ref, vn_scr, y_scr):
    tm = gu_ref.shape[0]
    vn_scr[...] = _layer_norm(gv_ref[...].astype(_F32), lng_ref[...], lnb_ref[...]).astype(_BF16)
    row = lax.broadcasted_iota(jnp.int32, (CHUNK, CHUNK), 0)
    col = lax.broadcasted_iota(jnp.int32, (CHUNK, CHUNK), 1)
    causal = row >= col
    for hd in range(SGU_HEADS):
        w = jnp.where(causal, ws_ref[hd], 0.0).astype(_BF16)
        bias = bs_ref[:, hd:hd + 1]
        cs = slice(hd * SGU_HEAD_DIM, (hd + 1) * SGU_HEAD_DIM)
        for c in range(tm // CHUNK):
            rs = slice(c * CHUNK, (c + 1) * CHUNK)
            mixed = jnp.dot(w, vn_scr[rs, cs], preferred_element_type=_F32) + bias
            y = gu_ref[rs, cs].astype(_F32) * mixed * sz_ref[rs, cs].astype(_F32)
            y_scr[rs, cs] = y.astype(_BF16)
    _out_proj_tail(y_scr, h_ref, wout_ref, plg_ref, plb_ref, of_ref, ob_ref)


def _sgu_out(guv, sz, h_f, ln_g, ln_b, w_s, b_s_t, w_out, pl_g, pl_b):
    tm = OUT_TM
    tile = (tm, D_INNER)
    return _out_call(
        _sgu_out_kernel, "sgu_out",
        [(guv, tile, lambda i: (i, 0)), (guv, tile, lambda i: (i, 1)), (sz, tile, lambda i: (i, 0)),
         (h_f, (tm, D_MODEL), lambda i: (i, 0))],
        [ln_g, ln_b, w_s, b_s_t, w_out, pl_g, pl_b],
        [pltpu.VMEM(tile, _BF16), pltpu.VMEM(tile, _BF16)])


def _pool_out_kernel(v_ref, vh_ref, sz_ref, h_ref, wpool_ref, scale_ref,
                     wout_ref, plg_ref, plb_ref, of_ref, ob_ref, y_scr):
    tm = v_ref.shape[0]
    pos0 = (pl.program_id(0) * tm) % SEQ
    seq_start = pos0 == 0
    gd = POOL_GROUP_DIM
    pos = pos0 + lax.broadcasted_iota(jnp.int32, (tm, gd), 0)
    for g, win in enumerate(POOL_WINDOWS):
        cs = slice(g * gd, (g + 1) * gd)
        halo = jnp.where(seq_start, 0.0, vh_ref[:, cs].astype(_F32))
        ext = jnp.concatenate([halo, v_ref[:, cs].astype(_F32)], axis=0)
        s, k = ext, 1
        while k < win:
            s = s + pltpu.roll(s, k, axis=0)
            k *= 2
        cnt = jnp.minimum(pos + 1, win).astype(_F32)
        p = s[POOL_HALO:] / cnt - ext[POOL_HALO:]
        q = jnp.dot(p.astype(_BF16), wpool_ref[g], preferred_element_type=_F32) * scale_ref[:, cs]
        y_scr[:, cs] = (q * sz_ref[:, cs].astype(_F32)).astype(_BF16)
    _out_proj_tail(y_scr, h_ref, wout_ref, plg_ref, plb_ref, of_ref, ob_ref)


def _pool_out(v, sz, h_f, w_pool, scale, w_out, pl_g, pl_b):
    tm = OUT_TM
    tile = (tm, D_INNER)
    halo_shape, halo_map = _halo_spec(POOL_HALO)
    return _out_call(
        _pool_out_kernel, "pool_out",
        [(v, tile, lambda i: (i, 0)), (v, halo_shape, halo_map), (sz, tile, lambda i: (i, 0)),
         (h_f, (tm, D_MODEL), lambda i: (i, 0))],
        [w_pool, scale, w_out, pl_g, pl_b],
        [pltpu.VMEM(tile, _BF16)])


def _conv_out_kernel(g_ref, gh_ref, sz_ref, h_ref, cw_ref, cb_ref, lng_ref, lnb_ref,
                     wout_ref, plg_ref, plb_ref, of_ref, ob_ref, ext_scr, c_scr, y_scr):
    tm = g_ref.shape[0]
    q = tm // CONV_ROW_STRIDE
    n_chunks = D_INNER // LANES
    seq_start = (pl.program_id(0) * tm) % SEQ == 0

    def conv_chunk(l, carry):
        ls = pl.ds(pl.multiple_of(l * LANES, LANES), LANES)
        ext_scr[0:CONV_HALO, :] = jnp.where(seq_start, 0.0, gh_ref[:, ls].astype(_F32))
        ext_scr[CONV_HALO:, :] = g_ref[:, ls].astype(_F32)
        accs = [jnp.broadcast_to(cb_ref[:, ls], (q, LANES))] * CONV_ROW_STRIDE
        for e in range(CONV_ROW_STRIDE - 1, -CONV_WIDTH, -1):
            slab = ext_scr[pl.ds(CONV_HALO + e, q, stride=CONV_ROW_STRIDE), :]
            for b in range(CONV_ROW_STRIDE):
                d = b - e
                if 0 <= d < CONV_WIDTH:
                    tap = CONV_WIDTH - 1 - d
                    accs[b] = accs[b] + slab * cw_ref[tap:tap + 1, ls]
        for b in range(CONV_ROW_STRIDE):
            c_scr[l, pl.ds(b, q, stride=CONV_ROW_STRIDE), :] = accs[b]
        return carry

    lax.fori_loop(0, n_chunks, conv_chunk, 0)

    c = c_scr[...]
    mu = jnp.sum(jnp.sum(c, axis=0), axis=-1, keepdims=True) * (1.0 / D_INNER)
    var = jnp.sum(jnp.sum(jnp.square(c - mu[None]), axis=0), axis=-1, keepdims=True) * (1.0 / D_INNER)
    rstd = lax.rsqrt(var + LN_EPS)
    for l in range(n_chunks):
        ls = slice(l * LANES, (l + 1) * LANES)
        s = _silu((c_scr[l] - mu) * rstd * lng_ref[:, ls] + lnb_ref[:, ls])
        y_scr[:, ls] = (s * sz_ref[:, ls].astype(_F32)).astype(_BF16)
    _out_proj_tail(y_scr, h_ref, wout_ref, plg_ref, plb_ref, of_ref, ob_ref)


def _conv_out(g, sz, h_f, conv_w, conv_b, ln_g, ln_b, w_out, pl_g, pl_b):
    tm = OUT_TM
    tile = (tm, D_INNER)
    halo_shape, halo_map = _halo_spec(CONV_HALO)
    return _out_call(
        _conv_out_kernel, "conv_out",
        [(g, tile, lambda i: (i, 0)), (g, halo_shape, halo_map), (sz, tile, lambda i: (i, 0)),
         (h_f, (tm, D_MODEL), lambda i: (i, 0))],
        [conv_w, conv_b, ln_g, ln_b, w_out, pl_g, pl_b],
        [pltpu.VMEM((CONV_HALO + tm, LANES), _F32), pltpu.VMEM((D_INNER // LANES, tm, LANES), _F32),
         pltpu.VMEM(tile, _BF16)])


def _row(v):
    return v.reshape(1, -1).astype(_F32)


def kernel(x, a_w_in, a_ln_g, a_ln_b, a_w_s, a_b_s, a_w_out, b_w_in, b_w_pool, b_scale, b_w_out,
           c_w_in, c_conv_w, c_conv_b, c_ln_g, c_ln_b, c_w_out, post_ln_g, post_ln_b):
    e_tiles = D_INNER // PROJ_TN
    h_f = x.reshape(N_TOKENS, D_MODEL)
    h_b = h_f.astype(_BF16)
    for i in range(DEPTH):
        kind, j = i % N_MIXERS, i // N_MIXERS
        pl_g, pl_b = _row(post_ln_g[i]), _row(post_ln_b[i])
        if kind == 0:
            w_out = a_w_out[j].astype(_BF16)
            guv = _proj(h_b, a_w_in, j, [0], 2 * e_tiles, _gelu, "sgu_in_uv")
            sz = _proj(h_b, a_w_in, j, [2 * e_tiles], e_tiles, _silu, "sgu_in_z")
            h_f, h_b = _sgu_out(guv, sz, h_f, _row(a_ln_g[j]), _row(a_ln_b[j]), a_w_s[j],
                                a_b_s[j].T, w_out, pl_g, pl_b)
        elif kind == 1:
            w_out = b_w_out[j].astype(_BF16)
            v = _proj(h_b, b_w_in, j, [0], e_tiles, lambda a: a, "pool_in_v")
            sz = _proj(h_b, b_w_in, j, [e_tiles], e_tiles, _silu, "pool_in_z")
            h_f, h_b = _pool_out(v, sz, h_f, b_w_pool[j].astype(_BF16), _row(b_scale[j]),
                                 w_out, pl_g, pl_b)
        else:
            w_out = c_w_out[j].astype(_BF16)
            g = _proj(h_b, c_w_in, j, [0, e_tiles], e_tiles, _glu, "conv_in_glu")
            sz = _proj(h_b, c_w_in, j, [2 * e_tiles], e_tiles, _silu, "conv_in_z")
            h_f, h_b = _conv_out(g, sz, h_f, c_conv_w[j], _row(c_conv_b[j]), _row(c_ln_g[j]),
                                 _row(c_ln_b[j]), w_out, pl_g, pl_b)
    return h_f.reshape(BATCH, SEQ, D_MODEL)
```

```python
import functools
import math

import jax
import jax.numpy as jnp
from jax import lax
from jax.experimental import pallas as pl
from jax.experimental.pallas import tpu as pltpu

D_MODEL = 2048
BATCH = 4
SEQ = 4096
DEPTH = 4
N_MIXERS = 3
D_INNER = 2 * D_MODEL
CHUNK = 128
SGU_HEADS = 8
SGU_HEAD_DIM = D_INNER // SGU_HEADS
POOL_WINDOWS = (2, 4, 8, 16)
POOL_GROUP_DIM = D_INNER // len(POOL_WINDOWS)
CONV_WIDTH = 31
LN_EPS = 1e-5
ALPHA = (2.0 * DEPTH) ** 0.25

N_TOKENS = BATCH * SEQ
POOL_HALO = 16
CONV_HALO = 32
LANES = 128
CONV_ROW_STRIDE = 4
CONV_ROWS = 256
GLU_TN = 512
GLU_SUB = 256

PROJ_TM = 1024
PROJ_TN = 1024
OUT_TM = 256
VMEM_LIMIT = 56 * 1024 * 1024

_F32 = jnp.float32
_BF16 = jnp.bfloat16
_GELU_C = math.sqrt(2.0 / math.pi)


def _gelu(x):
    hx = 0.5 * x
    return hx + hx * jnp.tanh(x * (_GELU_C + (_GELU_C * 0.044715) * (x * x)))


def _gated(x, g):
    hx = 0.5 * x
    return hx + hx * jnp.tanh(0.5 * g)


def _silu(x):
    return _gated(x, x)


def _glu(a, g):
    return _gated(a, g)


def _layer_norm(x, g, b):
    mu = jnp.mean(x, axis=-1, keepdims=True)
    xc = x - mu
    var = jnp.mean(xc * xc, axis=-1, keepdims=True)
    return xc * lax.rsqrt(var + LN_EPS) * g + b


def _proj_kernel(x_ref, *refs, n_w, epilogue):
    w_refs, o_ref, wb_refs = refs[:n_w], refs[n_w], refs[n_w + 1:]

    @pl.when(pl.program_id(1) == 0)
    def _():
        for w_ref, wb_ref in zip(w_refs, wb_refs):
            wb_ref[...] = w_ref[...].astype(_BF16)

    x = x_ref[...]
    accs = [jnp.dot(x, wb_ref[...], preferred_element_type=_F32) for wb_ref in wb_refs]
    o_ref[...] = epilogue(*accs).astype(o_ref.dtype)


def _proj(x, w, layer, col_tile_offsets, n_col_tiles, epilogue, name):
    n, k = x.shape
    tm, tn = PROJ_TM, PROJ_TN
    n_w = len(col_tile_offsets)
    w_mode = pl.Buffered(1) if n_w > 1 else None
    in_specs = [pl.BlockSpec((tm, k), lambda j, i: (i, 0))]
    for off in col_tile_offsets:
        in_specs.append(pl.BlockSpec((None, k, tn), lambda j, i, off=off: (layer, 0, j + off),
                                     pipeline_mode=w_mode))
    return pl.pallas_call(
        functools.partial(_proj_kernel, n_w=n_w, epilogue=epilogue),
        grid=(n_col_tiles, n // tm),
        in_specs=in_specs,
        out_specs=pl.BlockSpec((tm, tn), lambda j, i: (i, j)),
        out_shape=jax.ShapeDtypeStruct((n, n_col_tiles * tn), _BF16),
        scratch_shapes=[pltpu.VMEM((k, tn), _BF16)] * n_w,
        compiler_params=pltpu.CompilerParams(
            dimension_semantics=("arbitrary", "arbitrary"), vmem_limit_bytes=VMEM_LIMIT),
        name=name,
    )(x, *([w] * n_w))


def _glu_conv_kernel(x_ref, wa_ref, wg_ref, cw_ref, cb_ref, o_ref, wab_ref, wgb_ref, ext_scr, stage_scr,
                     *, tiles_per_seq):
    i = pl.program_id(1)
    tm, tn = o_ref.shape
    q = CONV_ROWS // CONV_ROW_STRIDE

    @pl.when(i == 0)
    def _():
        wab_ref[...] = wa_ref[...].astype(_BF16)
        wgb_ref[...] = wg_ref[...].astype(_BF16)

    @pl.when(i % tiles_per_seq == 0)
    def _():
        ext_scr[:, 0:CONV_HALO, :] = jnp.zeros((tn // LANES, CONV_HALO, LANES), _F32)

    x = x_ref[...]
    lanes_per_sub = GLU_SUB // LANES
    for s in range(tn // GLU_SUB):
        cs = slice(s * GLU_SUB, (s + 1) * GLU_SUB)
        a = jnp.dot(x, wab_ref[:, cs], preferred_element_type=_F32)
        gl = jnp.dot(x, wgb_ref[:, cs], preferred_element_type=_F32)
        g = _glu(a, gl)
        for l2 in range(lanes_per_sub):
            l = s * lanes_per_sub + l2
            ls = slice(l * LANES, (l + 1) * LANES)
            ext_scr[l, CONV_HALO:, :] = g[:, l2 * LANES:(l2 + 1) * LANES]
            for r0 in range(0, tm, CONV_ROWS):
                accs = [jnp.broadcast_to(cb_ref[:, ls], (q, LANES))] * CONV_ROW_STRIDE
                for e in range(CONV_ROW_STRIDE - 1, -CONV_WIDTH, -1):
                    slab = ext_scr[l, pl.ds(CONV_HALO + r0 + e, q, stride=CONV_ROW_STRIDE), :]
                    for b in range(CONV_ROW_STRIDE):
                        d = b - e
                        if 0 <= d < CONV_WIDTH:
                            tap = CONV_WIDTH - 1 - d
                            accs[b] = accs[b] + slab * cw_ref[tap:tap + 1, ls]
                for b in range(CONV_ROW_STRIDE):
                    stage_scr[l, pl.ds(r0 + b, q, stride=CONV_ROW_STRIDE), :] = accs[b]
            ext_scr[l, 0:CONV_HALO, :] = ext_scr[l, tm:tm + CONV_HALO, :]
            o_ref[:, ls] = stage_scr[l].astype(o_ref.dtype)


def _glu_conv(x, w, layer, conv_w, conv_b):
    n, k = x.shape
    tm, tn = PROJ_TM, GLU_TN
    n_tiles = D_INNER // tn
    return pl.pallas_call(
        functools.partial(_glu_conv_kernel, tiles_per_seq=SEQ // tm),
        grid=(n_tiles, n // tm),
        in_specs=[pl.BlockSpec((tm, k), lambda j, i: (i, 0)),
                  pl.BlockSpec((None, k, tn), lambda j, i: (layer, 0, j)),
                  pl.BlockSpec((None, k, tn), lambda j, i: (layer, 0, j + n_tiles)),
                  pl.BlockSpec((CONV_WIDTH, tn), lambda j, i: (0, j)),
                  pl.BlockSpec((1, tn), lambda j, i: (0, j))],
        out_specs=pl.BlockSpec((tm, tn), lambda j, i: (i, j)),
        out_shape=jax.ShapeDtypeStruct((n, D_INNER), _BF16),
        scratch_shapes=[pltpu.VMEM((k, tn), _BF16), pltpu.VMEM((k, tn), _BF16),
                        pltpu.VMEM((tn // LANES, CONV_HALO + tm, LANES), _F32),
                        pltpu.VMEM((tn // LANES, tm, LANES), _F32)],
        compiler_params=pltpu.CompilerParams(
            dimension_semantics=("arbitrary", "arbitrary"), vmem_limit_bytes=VMEM_LIMIT),
        name="conv_in_glu",
    )(x, w, w, conv_w, conv_b)


def _out_proj_tail(y_scr, h_ref, wout_ref, plg_ref, plb_ref, of_ref, ob_ref):
    acc = jnp.dot(y_scr[...], wout_ref[...], preferred_element_type=_F32)
    o = _layer_norm(ALPHA * h_ref[...] + acc, plg_ref[...], plb_ref[...])
    of_ref[...] = o
    ob_ref[...] = o.astype(_BF16)


def _const_spec(shape):
    return pl.BlockSpec(shape, lambda i: (0,) * len(shape), pipeline_mode=pl.Buffered(1))


def _out_call(kernel, name, tile_inputs, const_inputs, scratch_shapes):
    tm = OUT_TM
    in_specs = [pl.BlockSpec(bs, im) for _, bs, im in tile_inputs]
    in_specs += [_const_spec(a.shape) for a in const_inputs]
    out_spec = pl.BlockSpec((tm, D_MODEL), lambda i: (i, 0))
    return pl.pallas_call(
        kernel,
        grid=(N_TOKENS // tm,),
        in_specs=in_specs,
        out_specs=[out_spec, out_spec],
        out_shape=[jax.ShapeDtypeStruct((N_TOKENS, D_MODEL), _F32),
                   jax.ShapeDtypeStruct((N_TOKENS, D_MODEL), _BF16)],
        scratch_shapes=scratch_shapes,
        compiler_params=pltpu.CompilerParams(
            dimension_semantics=("arbitrary",), vmem_limit_bytes=VMEM_LIMIT),
        name=name,
    )(*[a for a, _, _ in tile_inputs], *const_inputs)


def _halo_spec(rows):
    per_tile = OUT_TM // rows
    return (rows, D_INNER), (lambda i: (jnp.maximum(i * per_tile - 1, 0), 0))


def _sgu_out_kernel(gu_ref, gv_ref, sz_ref, h_ref, lng_ref, lnb_ref, ws_ref, bs_ref,
                    wout_ref, plg_ref, plb_ref, of_ref, ob_ref, vn_scr, y_scr):
    tm = gu_ref.shape[0]
    vn_scr[...] = _layer_norm(gv_ref[...].astype(_F32), lng_ref[...], lnb_ref[...]).astype(_BF16)
    row = lax.broadcasted_iota(jnp.int32, (CHUNK, CHUNK), 0)
    col = lax.broadcasted_iota(jnp.int32, (CHUNK, CHUNK), 1)
    causal = row >= col
    for hd in range(SGU_HEADS):
        w = jnp.where(causal, ws_ref[hd], 0.0).astype(_BF16)
        bias = bs_ref[:, hd:hd + 1]
        cs = slice(hd * SGU_HEAD_DIM, (hd + 1) * SGU_HEAD_DIM)
        for c in range(tm // CHUNK):
            rs = slice(c * CHUNK, (c + 1) * CHUNK)
            mixed = jnp.dot(w, vn_scr[rs, cs], preferred_element_type=_F32) + bias
            y = gu_ref[rs, cs].astype(_F32) * mixed * sz_ref[rs, cs].astype(_F32)
            y_scr[rs, cs] = y.astype(_BF16)
    _out_proj_tail(y_scr, h_ref, wout_ref, plg_ref, plb_ref, of_ref, ob_ref)


def _sgu_out(guv, sz, h_f, ln_g, ln_b, w_s, b_s_t, w_out, pl_g, pl_b):
    tm = OUT_TM
    tile = (tm, D_INNER)
    return _out_call(
        _sgu_out_kernel, "sgu_out",
        [(guv, tile, lambda i: (i, 0)), (guv, tile, lambda i: (i, 1)), (sz, tile, lambda i: (i, 0)),
         (h_f, (tm, D_MODEL), lambda i: (i, 0))],
        [ln_g, ln_b, w_s, b_s_t, w_out, pl_g, pl_b],
        [pltpu.VMEM(tile, _BF16), pltpu.VMEM(tile, _BF16)])


def _pool_out_kernel(v_ref, vh_ref, sz_ref, h_ref, wpool_ref, scale_ref,
                     wout_ref, plg_ref, plb_ref, of_ref, ob_ref, y_scr):
    tm = v_ref.shape[0]
    pos0 = (pl.program_id(0) * tm) % SEQ
    seq_start = pos0 == 0
    gd = POOL_GROUP_DIM
    pos = pos0 + lax.broadcasted_iota(jnp.int32, (tm, gd), 0)
    for g, win in enumerate(POOL_WINDOWS):
        cs = slice(g * gd, (g + 1) * gd)
        halo = jnp.where(seq_start, 0.0, vh_ref[:, cs].astype(_F32))
        ext = jnp.concatenate([halo, v_ref[:, cs].astype(_F32)], axis=0)
        s, k = ext, 1
        while k < win:
            s = s + pltpu.roll(s, k, axis=0)
            k *= 2
        cnt = jnp.minimum(pos + 1, win).astype(_F32)
        p = s[POOL_HALO:] / cnt - ext[POOL_HALO:]
        q = jnp.dot(p.astype(_BF16), wpool_ref[g], preferred_element_type=_F32) * scale_ref[:, cs]
        y_scr[:, cs] = (q * sz_ref[:, cs].astype(_F32)).astype(_BF16)
    _out_proj_tail(y_scr, h_ref, wout_ref, plg_ref, plb_ref, of_ref, ob_ref)


def _pool_out(v, sz, h_f, w_pool, scale, w_out, pl_g, pl_b):
    tm = OUT_TM
    tile = (tm, D_INNER)
    halo_shape, halo_map = _halo_spec(POOL_HALO)
    return _out_call(
        _pool_out_kernel, "pool_out",
        [(v, tile, lambda i: (i, 0)), (v, halo_shape, halo_map), (sz, tile, lambda i: (i, 0)),
         (h_f, (tm, D_MODEL), lambda i: (i, 0))],
        [w_pool, scale, w_out, pl_g, pl_b],
        [pltpu.VMEM(tile, _BF16)])


def _conv_out_kernel(c_ref, sz_ref, h_ref, lng_ref, lnb_ref,
                     wout_ref, plg_ref, plb_ref, of_ref, ob_ref, y_scr):
    s = _silu(_layer_norm(c_ref[...].astype(_F32), lng_ref[...], lnb_ref[...]))
    y_scr[...] = (s * sz_ref[...].astype(_F32)).astype(_BF16)
    _out_proj_tail(y_scr, h_ref, wout_ref, plg_ref, plb_ref, of_ref, ob_ref)


def _conv_out(c, sz, h_f, ln_g, ln_b, w_out, pl_g, pl_b):
    tm = OUT_TM
    tile = (tm, D_INNER)
    return _out_call(
        _conv_out_kernel, "conv_out",
        [(c, tile, lambda i: (i, 0)), (sz, tile, lambda i: (i, 0)), (h_f, (tm, D_MODEL), lambda i: (i, 0))],
        [ln_g, ln_b, w_out, pl_g, pl_b],
        [pltpu.VMEM(tile, _BF16)])


def _row(v):
    return v.reshape(1, -1).astype(_F32)


def kernel(x, a_w_in, a_ln_g, a_ln_b, a_w_s, a_b_s, a_w_out, b_w_in, b_w_pool, b_scale, b_w_out,
           c_w_in, c_conv_w, c_conv_b, c_ln_g, c_ln_b, c_w_out, post_ln_g, post_ln_b):
    e_tiles = D_INNER // PROJ_TN
    h_f = x.reshape(N_TOKENS, D_MODEL)
    h_b = h_f.astype(_BF16)
    for i in range(DEPTH):
        kind, j = i % N_MIXERS, i // N_MIXERS
        pl_g, pl_b = _row(post_ln_g[i]), _row(post_ln_b[i])
        if kind == 0:
            w_out = a_w_out[j].astype(_BF16)
            guv = _proj(h_b, a_w_in, j, [0], 2 * e_tiles, _gelu, "sgu_in_uv")
            sz = _proj(h_b, a_w_in, j, [2 * e_tiles], e_tiles, _silu, "sgu_in_z")
            h_f, h_b = _sgu_out(guv, sz, h_f, _row(a_ln_g[j]), _row(a_ln_b[j]), a_w_s[j],
                                a_b_s[j].T, w_out, pl_g, pl_b)
        elif kind == 1:
            w_out = b_w_out[j].astype(_BF16)
            v = _proj(h_b, b_w_in, j, [0], e_tiles, lambda a: a, "pool_in_v")
            sz = _proj(h_b, b_w_in, j, [e_tiles], e_tiles, _silu, "pool_in_z")
            h_f, h_b = _pool_out(v, sz, h_f, b_w_pool[j].astype(_BF16), _row(b_scale[j]),
                                 w_out, pl_g, pl_b)
        else:
            w_out = c_w_out[j].astype(_BF16)
            c = _glu_conv(h_b, c_w_in, j, c_conv_w[j], _row(c_conv_b[j]))
            sz = _proj(h_b, c_w_in, j, [2 * e_tiles], e_tiles, _silu, "conv_in_z")
            h_f, h_b = _conv_out(c, sz, h_f, _row(c_ln_g[j]), _row(c_ln_b[j]), w_out, pl_g, pl_b)
    return h_f.reshape(BATCH, SEQ, D_MODEL)
```

```python
import functools
import math

import jax
import jax.numpy as jnp
from jax import lax
from jax.experimental import pallas as pl
from jax.experimental.pallas import tpu as pltpu

D_MODEL = 2048
BATCH = 4
SEQ = 4096
DEPTH = 4
N_MIXERS = 3
D_INNER = 2 * D_MODEL
CHUNK = 128
SGU_HEADS = 8
SGU_HEAD_DIM = D_INNER // SGU_HEADS
POOL_WINDOWS = (2, 4, 8, 16)
POOL_GROUP_DIM = D_INNER // len(POOL_WINDOWS)
CONV_WIDTH = 31
LN_EPS = 1e-5
ALPHA = (2.0 * DEPTH) ** 0.25

N_TOKENS = BATCH * SEQ
POOL_HALO = 16
CONV_HALO = 32
LANES = 128
CONV_ROW_STRIDE = 4
CONV_ROWS = 256
GLU_TN = 512
GLU_SUB = 256

PROJ_TM = 1024
PROJ_TN = 1024
OUT_TM = 256
CAST_ROWS = 512
VMEM_LIMIT = 56 * 1024 * 1024

_F32 = jnp.float32
_BF16 = jnp.bfloat16
_GELU_C = math.sqrt(2.0 / math.pi)


def _gelu(x):
    hx = 0.5 * x
    return hx + hx * jnp.tanh(x * (_GELU_C + (_GELU_C * 0.044715) * (x * x)))


def _gated(x, g):
    hx = 0.5 * x
    return hx + hx * jnp.tanh(0.5 * g)


def _silu(x):
    return _gated(x, x)


def _glu(a, g):
    return _gated(a, g)


def _layer_norm(x, g, b):
    mu = jnp.mean(x, axis=-1, keepdims=True)
    xc = x - mu
    var = jnp.mean(xc * xc, axis=-1, keepdims=True)
    return xc * lax.rsqrt(var + LN_EPS) * g + b


def _proj_kernel(x_ref, *refs, n_w, epilogue):
    w_refs, o_ref, wb_refs = refs[:n_w], refs[n_w], refs[n_w + 1:]

    @pl.when(pl.program_id(1) == 0)
    def _():
        for w_ref, wb_ref in zip(w_refs, wb_refs):
            wb_ref[...] = w_ref[...].astype(_BF16)

    x = x_ref[...]
    accs = [jnp.dot(x, wb_ref[...], preferred_element_type=_F32) for wb_ref in wb_refs]
    o_ref[...] = epilogue(*accs).astype(o_ref.dtype)


def _proj(x, w, layer, col_tile_offsets, n_col_tiles, epilogue, name):
    n, k = x.shape
    tm, tn = PROJ_TM, PROJ_TN
    n_w = len(col_tile_offsets)
    w_mode = pl.Buffered(1) if n_w > 1 else None
    in_specs = [pl.BlockSpec((tm, k), lambda j, i: (i, 0))]
    for off in col_tile_offsets:
        in_specs.append(pl.BlockSpec((None, k, tn), lambda j, i, off=off: (layer, 0, j + off),
                                     pipeline_mode=w_mode))
    return pl.pallas_call(
        functools.partial(_proj_kernel, n_w=n_w, epilogue=epilogue),
        grid=(n_col_tiles, n // tm),
        in_specs=in_specs,
        out_specs=pl.BlockSpec((tm, tn), lambda j, i: (i, j)),
        out_shape=jax.ShapeDtypeStruct((n, n_col_tiles * tn), _BF16),
        scratch_shapes=[pltpu.VMEM((k, tn), _BF16)] * n_w,
        compiler_params=pltpu.CompilerParams(
            dimension_semantics=("arbitrary", "arbitrary"), vmem_limit_bytes=VMEM_LIMIT),
        name=name,
    )(x, *([w] * n_w))


def _glu_conv_kernel(x_ref, wa_ref, wg_ref, cw_ref, cb_ref, o_ref, wab_ref, wgb_ref, ext_scr, stage_scr,
                     *, tiles_per_seq):
    i = pl.program_id(1)
    tm, tn = o_ref.shape
    q = CONV_ROWS // CONV_ROW_STRIDE

    @pl.when(i == 0)
    def _():
        wab_ref[...] = wa_ref[...].astype(_BF16)
        wgb_ref[...] = wg_ref[...].astype(_BF16)

    @pl.when(i % tiles_per_seq == 0)
    def _():
        ext_scr[:, 0:CONV_HALO, :] = jnp.zeros((tn // LANES, CONV_HALO, LANES), _F32)

    x = x_ref[...]
    lanes_per_sub = GLU_SUB // LANES
    for s in range(tn // GLU_SUB):
        cs = slice(s * GLU_SUB, (s + 1) * GLU_SUB)
        a = jnp.dot(x, wab_ref[:, cs], preferred_element_type=_F32)
        gl = jnp.dot(x, wgb_ref[:, cs], preferred_element_type=_F32)
        g = _glu(a, gl)
        for l2 in range(lanes_per_sub):
            l = s * lanes_per_sub + l2
            ls = slice(l * LANES, (l + 1) * LANES)
            ext_scr[l, CONV_HALO:, :] = g[:, l2 * LANES:(l2 + 1) * LANES]
            for r0 in range(0, tm, CONV_ROWS):
                accs = [jnp.broadcast_to(cb_ref[:, ls], (q, LANES))] * CONV_ROW_STRIDE
                for e in range(CONV_ROW_STRIDE - 1, -CONV_WIDTH, -1):
                    slab = ext_scr[l, pl.ds(CONV_HALO + r0 + e, q, stride=CONV_ROW_STRIDE), :]
                    for b in range(CONV_ROW_STRIDE):
                        d = b - e
                        if 0 <= d < CONV_WIDTH:
                            tap = CONV_WIDTH - 1 - d
                            accs[b] = accs[b] + slab * cw_ref[tap:tap + 1, ls]
                for b in range(CONV_ROW_STRIDE):
                    stage_scr[l, pl.ds(r0 + b, q, stride=CONV_ROW_STRIDE), :] = accs[b]
            ext_scr[l, 0:CONV_HALO, :] = ext_scr[l, tm:tm + CONV_HALO, :]
            o_ref[:, ls] = stage_scr[l].astype(o_ref.dtype)


def _glu_conv(x, w, layer, conv_w, conv_b):
    n, k = x.shape
    tm, tn = PROJ_TM, GLU_TN
    n_tiles = D_INNER // tn
    return pl.pallas_call(
        functools.partial(_glu_conv_kernel, tiles_per_seq=SEQ // tm),
        grid=(n_tiles, n // tm),
        in_specs=[pl.BlockSpec((tm, k), lambda j, i: (i, 0)),
                  pl.BlockSpec((None, k, tn), lambda j, i: (layer, 0, j)),
                  pl.BlockSpec((None, k, tn), lambda j, i: (layer, 0, j + n_tiles)),
                  pl.BlockSpec((CONV_WIDTH, tn), lambda j, i: (0, j)),
                  pl.BlockSpec((1, tn), lambda j, i: (0, j))],
        out_specs=pl.BlockSpec((tm, tn), lambda j, i: (i, j)),
        out_shape=jax.ShapeDtypeStruct((n, D_INNER), _BF16),
        scratch_shapes=[pltpu.VMEM((k, tn), _BF16), pltpu.VMEM((k, tn), _BF16),
                        pltpu.VMEM((tn // LANES, CONV_HALO + tm, LANES), _F32),
                        pltpu.VMEM((tn // LANES, tm, LANES), _F32)],
        compiler_params=pltpu.CompilerParams(
            dimension_semantics=("arbitrary", "arbitrary"), vmem_limit_bytes=VMEM_LIMIT),
        name="conv_in_glu",
    )(x, w, w, conv_w, conv_b)


def _out_proj_tail(y_scr, h_ref, wout_ref, plg_ref, plb_ref, of_ref, ob_ref):
    acc = jnp.dot(y_scr[...], wout_ref[...], preferred_element_type=_F32)
    o = _layer_norm(ALPHA * h_ref[...] + acc, plg_ref[...], plb_ref[...])
    of_ref[...] = o
    ob_ref[...] = o.astype(_BF16)


def _const_spec(shape):
    return pl.BlockSpec(shape, lambda i: (0,) * len(shape), pipeline_mode=pl.Buffered(1))


def _out_call(kernel, name, tile_inputs, const_inputs, scratch_shapes):
    tm = OUT_TM
    in_specs = [pl.BlockSpec(bs, im) for _, bs, im in tile_inputs]
    in_specs += [_const_spec(a.shape) for a in const_inputs]
    out_spec = pl.BlockSpec((tm, D_MODEL), lambda i: (i, 0))
    return pl.pallas_call(
        kernel,
        grid=(N_TOKENS // tm,),
        in_specs=in_specs,
        out_specs=[out_spec, out_spec],
        out_shape=[jax.ShapeDtypeStruct((N_TOKENS, D_MODEL), _F32),
                   jax.ShapeDtypeStruct((N_TOKENS, D_MODEL), _BF16)],
        scratch_shapes=scratch_shapes,
        compiler_params=pltpu.CompilerParams(
            dimension_semantics=("arbitrary",), vmem_limit_bytes=VMEM_LIMIT),
        name=name,
    )(*[a for a, _, _ in tile_inputs], *const_inputs)


def _halo_spec(rows):
    per_tile = OUT_TM // rows
    return (rows, D_INNER), (lambda i: (jnp.maximum(i * per_tile - 1, 0), 0))


def _sgu_out_kernel(guz_ref, gv_ref, h_ref, lng_ref, lnb_ref, ws_ref, bs_ref,
                    wout_ref, plg_ref, plb_ref, of_ref, ob_ref, vn_scr, y_scr):
    tm = guz_ref.shape[0]
    vn_scr[...] = _layer_norm(gv_ref[...].astype(_F32), lng_ref[...], lnb_ref[...]).astype(_BF16)
    row = lax.broadcasted_iota(jnp.int32, (CHUNK, CHUNK), 0)
    col = lax.broadcasted_iota(jnp.int32, (CHUNK, CHUNK), 1)
    causal = row >= col
    for hd in range(SGU_HEADS):
        w = jnp.where(causal, ws_ref[hd], 0.0).astype(_BF16)
        bias = bs_ref[:, hd:hd + 1]
        cs = slice(hd * SGU_HEAD_DIM, (hd + 1) * SGU_HEAD_DIM)
        for c in range(tm // CHUNK):
            rs = slice(c * CHUNK, (c + 1) * CHUNK)
            mixed = jnp.dot(w, vn_scr[rs, cs], preferred_element_type=_F32) + bias
            y_scr[rs, cs] = (guz_ref[rs, cs].astype(_F32) * mixed).astype(_BF16)
    _out_proj_tail(y_scr, h_ref, wout_ref, plg_ref, plb_ref, of_ref, ob_ref)


def _sgu_out(guz, gv, h_f, ln_g, ln_b, w_s, b_s_t, w_out, pl_g, pl_b):
    tm = OUT_TM
    tile = (tm, D_INNER)
    return _out_call(
        _sgu_out_kernel, "sgu_out",
        [(guz, tile, lambda i: (i, 0)), (gv, tile, lambda i: (i, 0)),
         (h_f, (tm, D_MODEL), lambda i: (i, 0))],
        [ln_g, ln_b, w_s, b_s_t, w_out, pl_g, pl_b],
        [pltpu.VMEM(tile, _BF16), pltpu.VMEM(tile, _BF16)])


def _pool_out_kernel(v_ref, vh_ref, sz_ref, h_ref, wpool_ref, scale_ref,
                     wout_ref, plg_ref, plb_ref, of_ref, ob_ref, y_scr):
    tm = v_ref.shape[0]
    pos0 = (pl.program_id(0) * tm) % SEQ
    seq_start = pos0 == 0
    gd = POOL_GROUP_DIM
    pos = pos0 + lax.broadcasted_iota(jnp.int32, (tm, gd), 0)
    for g, win in enumerate(POOL_WINDOWS):
        cs = slice(g * gd, (g + 1) * gd)
        halo = jnp.where(seq_start, 0.0, vh_ref[:, cs].astype(_F32))
        ext = jnp.concatenate([halo, v_ref[:, cs].astype(_F32)], axis=0)
        s, k = ext, 1
        while k < win:
            s = s + pltpu.roll(s, k, axis=0)
            k *= 2
        cnt = jnp.minimum(pos + 1, win).astype(_F32)
        p = s[POOL_HALO:] / cnt - ext[POOL_HALO:]
        q = jnp.dot(p.astype(_BF16), wpool_ref[g], preferred_element_type=_F32) * scale_ref[:, cs]
        y_scr[:, cs] = (q * sz_ref[:, cs].astype(_F32)).astype(_BF16)
    _out_proj_tail(y_scr, h_ref, wout_ref, plg_ref, plb_ref, of_ref, ob_ref)


def _pool_out(v, sz, h_f, w_pool, scale, w_out, pl_g, pl_b):
    tm = OUT_TM
    tile = (tm, D_INNER)
    halo_shape, halo_map = _halo_spec(POOL_HALO)
    return _out_call(
        _pool_out_kernel, "pool_out",
        [(v, tile, lambda i: (i, 0)), (v, halo_shape, halo_map), (sz, tile, lambda i: (i, 0)),
         (h_f, (tm, D_MODEL), lambda i: (i, 0))],
        [w_pool, scale, w_out, pl_g, pl_b],
        [pltpu.VMEM(tile, _BF16)])


def _conv_out_kernel(c_ref, sz_ref, h_ref, lng_ref, lnb_ref,
                     wout_ref, plg_ref, plb_ref, of_ref, ob_ref, y_scr):
    s = _silu(_layer_norm(c_ref[...].astype(_F32), lng_ref[...], lnb_ref[...]))
    y_scr[...] = (s * sz_ref[...].astype(_F32)).astype(_BF16)
    _out_proj_tail(y_scr, h_ref, wout_ref, plg_ref, plb_ref, of_ref, ob_ref)


def _conv_out(c, sz, h_f, ln_g, ln_b, w_out, pl_g, pl_b):
    tm = OUT_TM
    tile = (tm, D_INNER)
    return _out_call(
        _conv_out_kernel, "conv_out",
        [(c, tile, lambda i: (i, 0)), (sz, tile, lambda i: (i, 0)), (h_f, (tm, D_MODEL), lambda i: (i, 0))],
        [ln_g, ln_b, w_out, pl_g, pl_b],
        [pltpu.VMEM(tile, _BF16)])


def _row(v):
    return v.reshape(1, -1).astype(_F32)


def _cast_kernel(w_ref, o_ref):
    o_ref[...] = w_ref[...].astype(o_ref.dtype)


def _to_bf16(w, layer):
    _, rows, cols = w.shape
    return pl.pallas_call(
        _cast_kernel,
        grid=(rows // CAST_ROWS,),
        in_specs=[pl.BlockSpec((None, CAST_ROWS, cols), lambda r: (layer, r, 0))],
        out_specs=pl.BlockSpec((CAST_ROWS, cols), lambda r: (r, 0)),
        out_shape=jax.ShapeDtypeStruct((rows, cols), _BF16),
        compiler_params=pltpu.CompilerParams(dimension_semantics=("arbitrary",)),
        name="weight_cast",
    )(w)


def kernel(x, a_w_in, a_ln_g, a_ln_b, a_w_s, a_b_s, a_w_out, b_w_in, b_w_pool, b_scale, b_w_out,
           c_w_in, c_conv_w, c_conv_b, c_ln_g, c_ln_b, c_w_out, post_ln_g, post_ln_b):
    e_tiles = D_INNER // PROJ_TN
    h_f = x.reshape(N_TOKENS, D_MODEL)
    h_b = h_f.astype(_BF16)
    for i in range(DEPTH):
        kind, j = i % N_MIXERS, i // N_MIXERS
        pl_g, pl_b = _row(post_ln_g[i]), _row(post_ln_b[i])
        if kind == 0:
            guz = _proj(h_b, a_w_in, j, [0, 2 * e_tiles], e_tiles,
                        lambda u, z: _gelu(u) * _silu(z), "sgu_in_uz")
            gv = _proj(h_b, a_w_in, j, [e_tiles], e_tiles, _gelu, "sgu_in_v")
            h_f, h_b = _sgu_out(guz, gv, h_f, _row(a_ln_g[j]), _row(a_ln_b[j]), a_w_s[j],
                                a_b_s[j].T, _to_bf16(a_w_out, j), pl_g, pl_b)
        elif kind == 1:
            v = _proj(h_b, b_w_in, j, [0], e_tiles, lambda a: a, "pool_in_v")
            sz = _proj(h_b, b_w_in, j, [e_tiles], e_tiles, _silu, "pool_in_z")
            w_pool = _to_bf16(b_w_pool.reshape(b_w_pool.shape[0], D_INNER, POOL_GROUP_DIM), j)
            h_f, h_b = _pool_out(v, sz, h_f, w_pool.reshape(b_w_pool.shape[1:]), _row(b_scale[j]),
                                 _to_bf16(b_w_out, j), pl_g, pl_b)
        else:
            w_out = _to_bf16(c_w_out, j)
            c = _glu_conv(h_b, c_w_in, j, c_conv_w[j], _row(c_conv_b[j]))
            sz = _proj(h_b, c_w_in, j, [2 * e_tiles], e_tiles, _silu, "conv_in_z")
            h_f, h_b = _conv_out(c, sz, h_f, _row(c_ln_g[j]), _row(c_ln_b[j]), w_out, pl_g, pl_b)
    return h_f.reshape(BATCH, SEQ, D_MODEL)
```

```python
import functools
import math

import jax
import jax.numpy as jnp
from jax import lax
from jax.experimental import pallas as pl
from jax.experimental.pallas import tpu as pltpu

D_MODEL = 2048
BATCH = 4
SEQ = 4096
DEPTH = 4
N_MIXERS = 3
D_INNER = 2 * D_MODEL
CHUNK = 128
SGU_HEADS = 8
SGU_HEAD_DIM = D_INNER // SGU_HEADS
POOL_WINDOWS = (2, 4, 8, 16)
POOL_GROUP_DIM = D_INNER // len(POOL_WINDOWS)
CONV_WIDTH = 31
LN_EPS = 1e-5
ALPHA = (2.0 * DEPTH) ** 0.25

N_TOKENS = BATCH * SEQ
POOL_HALO = 16
CONV_HALO = 32
LANES = 128
CONV_ROW_STRIDE = 4
CONV_ROWS = 64
GLU_TN = 512
GLU_SUB = 256

PROJ_TM = 1024
PROJ_TN = 1024
OUT_TM = 256
CAST_ROWS = 512
VMEM_LIMIT = 56 * 1024 * 1024

_F32 = jnp.float32
_BF16 = jnp.bfloat16
_GELU_C = math.sqrt(2.0 / math.pi)


def _gelu(x):
    hx = 0.5 * x
    return hx + hx * jnp.tanh(x * (_GELU_C + (_GELU_C * 0.044715) * (x * x)))


def _gated(x, g):
    hx = 0.5 * x
    return hx + hx * jnp.tanh(0.5 * g)


def _silu(x):
    return _gated(x, x)


def _glu(a, g):
    return _gated(a, g)


def _layer_norm(x, g, b):
    mu = jnp.mean(x, axis=-1, keepdims=True)
    xc = x - mu
    var = jnp.mean(xc * xc, axis=-1, keepdims=True)
    return xc * lax.rsqrt(var + LN_EPS) * g + b


def _proj_kernel(x_ref, *refs, n_w, epilogue):
    w_refs, o_ref, wb_refs = refs[:n_w], refs[n_w], refs[n_w + 1:]

    @pl.when(pl.program_id(1) == 0)
    def _():
        for w_ref, wb_ref in zip(w_refs, wb_refs):
            wb_ref[...] = w_ref[...].astype(_BF16)

    x = x_ref[...]
    accs = [jnp.dot(x, wb_ref[...], preferred_element_type=_F32) for wb_ref in wb_refs]
    o_ref[...] = epilogue(*accs).astype(o_ref.dtype)


def _proj(x, w, layer, col_tile_offsets, n_col_tiles, epilogue, name):
    n, k = x.shape
    tm, tn = PROJ_TM, PROJ_TN
    n_w = len(col_tile_offsets)
    w_mode = pl.Buffered(1) if n_w > 1 else None
    in_specs = [pl.BlockSpec((tm, k), lambda j, i: (i, 0))]
    for off in col_tile_offsets:
        in_specs.append(pl.BlockSpec((None, k, tn), lambda j, i, off=off: (layer, 0, j + off),
                                     pipeline_mode=w_mode))
    return pl.pallas_call(
        functools.partial(_proj_kernel, n_w=n_w, epilogue=epilogue),
        grid=(n_col_tiles, n // tm),
        in_specs=in_specs,
        out_specs=pl.BlockSpec((tm, tn), lambda j, i: (i, j)),
        out_shape=jax.ShapeDtypeStruct((n, n_col_tiles * tn), _BF16),
        scratch_shapes=[pltpu.VMEM((k, tn), _BF16)] * n_w,
        compiler_params=pltpu.CompilerParams(
            dimension_semantics=("arbitrary", "arbitrary"), vmem_limit_bytes=VMEM_LIMIT),
        name=name,
    )(x, *([w] * n_w))


def _glu_conv_kernel(x_ref, wa_ref, wg_ref, cw_ref, cb_ref, o_ref, wab_ref, wgb_ref, ext_scr, stage_scr,
                     *, tiles_per_seq):
    i = pl.program_id(1)
    tm, tn = o_ref.shape
    q = CONV_ROWS // CONV_ROW_STRIDE

    @pl.when(i == 0)
    def _():
        wab_ref[...] = wa_ref[...].astype(_BF16)
        wgb_ref[...] = wg_ref[...].astype(_BF16)

    @pl.when(i % tiles_per_seq == 0)
    def _():
        ext_scr[:, 0:CONV_HALO, :] = jnp.zeros((tn // LANES, CONV_HALO, LANES), _F32)

    x = x_ref[...]
    lanes_per_sub = GLU_SUB // LANES
    for s in range(tn // GLU_SUB):
        cs = slice(s * GLU_SUB, (s + 1) * GLU_SUB)
        a = jnp.dot(x, wab_ref[:, cs], preferred_element_type=_F32)
        gl = jnp.dot(x, wgb_ref[:, cs], preferred_element_type=_F32)
        g = _glu(a, gl)
        for l2 in range(lanes_per_sub):
            l = s * lanes_per_sub + l2
            ls = slice(l * LANES, (l + 1) * LANES)
            ext_scr[l, CONV_HALO:, :] = g[:, l2 * LANES:(l2 + 1) * LANES]
            for r0 in range(0, tm, CONV_ROWS):
                accs = [jnp.broadcast_to(cb_ref[:, ls], (q, LANES))] * CONV_ROW_STRIDE
                for e in range(CONV_ROW_STRIDE - 1, -CONV_WIDTH, -1):
                    slab = ext_scr[l, pl.ds(CONV_HALO + r0 + e, q, stride=CONV_ROW_STRIDE), :]
                    for b in range(CONV_ROW_STRIDE):
                        d = b - e
                        if 0 <= d < CONV_WIDTH:
                            tap = CONV_WIDTH - 1 - d
                            accs[b] = accs[b] + slab * cw_ref[tap:tap + 1, ls]
                for b in range(CONV_ROW_STRIDE):
                    stage_scr[l, pl.ds(r0 + b, q, stride=CONV_ROW_STRIDE), :] = accs[b]
            ext_scr[l, 0:CONV_HALO, :] = ext_scr[l, tm:tm + CONV_HALO, :]
            o_ref[:, ls] = stage_scr[l].astype(o_ref.dtype)


def _glu_conv(x, w, layer, conv_w, conv_b):
    n, k = x.shape
    tm, tn = PROJ_TM, GLU_TN
    n_tiles = D_INNER // tn
    return pl.pallas_call(
        functools.partial(_glu_conv_kernel, tiles_per_seq=SEQ // tm),
        grid=(n_tiles, n // tm),
        in_specs=[pl.BlockSpec((tm, k), lambda j, i: (i, 0)),
                  pl.BlockSpec((None, k, tn), lambda j, i: (layer, 0, j)),
                  pl.BlockSpec((None, k, tn), lambda j, i: (layer, 0, j + n_tiles)),
                  pl.BlockSpec((CONV_WIDTH, tn), lambda j, i: (0, j)),
                  pl.BlockSpec((1, tn), lambda j, i: (0, j))],
        out_specs=pl.BlockSpec((tm, tn), lambda j, i: (i, j)),
        out_shape=jax.ShapeDtypeStruct((n, D_INNER), _BF16),
        scratch_shapes=[pltpu.VMEM((k, tn), _BF16), pltpu.VMEM((k, tn), _BF16),
                        pltpu.VMEM((tn // LANES, CONV_HALO + tm, LANES), _F32),
                        pltpu.VMEM((tn // LANES, tm, LANES), _F32)],
        compiler_params=pltpu.CompilerParams(
            dimension_semantics=("arbitrary", "arbitrary"), vmem_limit_bytes=VMEM_LIMIT),
        name="conv_in_glu",
    )(x, w, w, conv_w, conv_b)


def _out_proj_tail(y_scr, h_ref, wout_ref, plg_ref, plb_ref, of_ref, ob_ref):
    acc = jnp.dot(y_scr[...], wout_ref[...], preferred_element_type=_F32)
    o = _layer_norm(ALPHA * h_ref[...] + acc, plg_ref[...], plb_ref[...])
    of_ref[...] = o
    ob_ref[...] = o.astype(_BF16)


def _const_spec(shape):
    return pl.BlockSpec(shape, lambda i: (0,) * len(shape), pipeline_mode=pl.Buffered(1))


def _out_call(kernel, name, tile_inputs, const_inputs, scratch_shapes):
    tm = OUT_TM
    in_specs = [pl.BlockSpec(bs, im) for _, bs, im in tile_inputs]
    in_specs += [_const_spec(a.shape) for a in const_inputs]
    out_spec = pl.BlockSpec((tm, D_MODEL), lambda i: (i, 0))
    return pl.pallas_call(
        kernel,
        grid=(N_TOKENS // tm,),
        in_specs=in_specs,
        out_specs=[out_spec, out_spec],
        out_shape=[jax.ShapeDtypeStruct((N_TOKENS, D_MODEL), _F32),
                   jax.ShapeDtypeStruct((N_TOKENS, D_MODEL), _BF16)],
        scratch_shapes=scratch_shapes,
        compiler_params=pltpu.CompilerParams(
            dimension_semantics=("arbitrary",), vmem_limit_bytes=VMEM_LIMIT),
        name=name,
    )(*[a for a, _, _ in tile_inputs], *const_inputs)


def _halo_spec(rows):
    per_tile = OUT_TM // rows
    return (rows, D_INNER), (lambda i: (jnp.maximum(i * per_tile - 1, 0), 0))


def _sgu_out_kernel(guz_ref, gv_ref, h_ref, lng_ref, lnb_ref, ws_ref, bs_ref,
                    wout_ref, plg_ref, plb_ref, of_ref, ob_ref, vn_scr, y_scr):
    tm = guz_ref.shape[0]
    vn_scr[...] = _layer_norm(gv_ref[...].astype(_F32), lng_ref[...], lnb_ref[...]).astype(_BF16)
    row = lax.broadcasted_iota(jnp.int32, (CHUNK, CHUNK), 0)
    col = lax.broadcasted_iota(jnp.int32, (CHUNK, CHUNK), 1)
    causal = row >= col
    for hd in range(SGU_HEADS):
        w = jnp.where(causal, ws_ref[hd], 0.0).astype(_BF16)
        bias = bs_ref[:, hd:hd + 1]
        cs = slice(hd * SGU_HEAD_DIM, (hd + 1) * SGU_HEAD_DIM)
        for c in range(tm // CHUNK):
            rs = slice(c * CHUNK, (c + 1) * CHUNK)
            mixed = jnp.dot(w, vn_scr[rs, cs], preferred_element_type=_F32) + bias
            y_scr[rs, cs] = (guz_ref[rs, cs].astype(_F32) * mixed).astype(_BF16)
    _out_proj_tail(y_scr, h_ref, wout_ref, plg_ref, plb_ref, of_ref, ob_ref)


def _sgu_out(guz, gv, h_f, ln_g, ln_b, w_s, b_s_t, w_out, pl_g, pl_b):
    tm = OUT_TM
    tile = (tm, D_INNER)
    return _out_call(
        _sgu_out_kernel, "sgu_out",
        [(guz, tile, lambda i: (i, 0)), (gv, tile, lambda i: (i, 0)),
         (h_f, (tm, D_MODEL), lambda i: (i, 0))],
        [ln_g, ln_b, w_s, b_s_t, w_out, pl_g, pl_b],
        [pltpu.VMEM(tile, _BF16), pltpu.VMEM(tile, _BF16)])


def _pool_out_kernel(v_ref, vh_ref, sz_ref, h_ref, wpool_ref, scale_ref,
                     wout_ref, plg_ref, plb_ref, of_ref, ob_ref, y_scr):
    tm = v_ref.shape[0]
    pos0 = (pl.program_id(0) * tm) % SEQ
    seq_start = pos0 == 0
    gd = POOL_GROUP_DIM
    pos = pos0 + lax.broadcasted_iota(jnp.int32, (tm, gd), 0)
    for g, win in enumerate(POOL_WINDOWS):
        cs = slice(g * gd, (g + 1) * gd)
        halo = jnp.where(seq_start, 0.0, vh_ref[:, cs].astype(_F32))
        ext = jnp.concatenate([halo, v_ref[:, cs].astype(_F32)], axis=0)
        s, k = ext, 1
        while k < win:
            s = s + pltpu.roll(s, k, axis=0)
            k *= 2
        cnt = jnp.minimum(pos + 1, win).astype(_F32)
        p = s[POOL_HALO:] / cnt - ext[POOL_HALO:]
        q = jnp.dot(p.astype(_BF16), wpool_ref[g], preferred_element_type=_F32) * scale_ref[:, cs]
        y_scr[:, cs] = (q * sz_ref[:, cs].astype(_F32)).astype(_BF16)
    _out_proj_tail(y_scr, h_ref, wout_ref, plg_ref, plb_ref, of_ref, ob_ref)


def _pool_out(v, sz, h_f, w_pool, scale, w_out, pl_g, pl_b):
    tm = OUT_TM
    tile = (tm, D_INNER)
    halo_shape, halo_map = _halo_spec(POOL_HALO)
    return _out_call(
        _pool_out_kernel, "pool_out",
        [(v, tile, lambda i: (i, 0)), (v, halo_shape, halo_map), (sz, tile, lambda i: (i, 0)),
         (h_f, (tm, D_MODEL), lambda i: (i, 0))],
        [w_pool, scale, w_out, pl_g, pl_b],
        [pltpu.VMEM(tile, _BF16)])


def _conv_out_kernel(c_ref, sz_ref, h_ref, lng_ref, lnb_ref,
                     wout_ref, plg_ref, plb_ref, of_ref, ob_ref, y_scr):
    s = _silu(_layer_norm(c_ref[...].astype(_F32), lng_ref[...], lnb_ref[...]))
    y_scr[...] = (s * sz_ref[...].astype(_F32)).astype(_BF16)
    _out_proj_tail(y_scr, h_ref, wout_ref, plg_ref, plb_ref, of_ref, ob_ref)


def _conv_out(c, sz, h_f, ln_g, ln_b, w_out, pl_g, pl_b):
    tm = OUT_TM
    tile = (tm, D_INNER)
    return _out_call(
        _conv_out_kernel, "conv_out",
        [(c, tile, lambda i: (i, 0)), (sz, tile, lambda i: (i, 0)), (h_f, (tm, D_MODEL), lambda i: (i, 0))],
        [ln_g, ln_b, w_out, pl_g, pl_b],
        [pltpu.VMEM(tile, _BF16)])


def _row(v):
    return v.reshape(1, -1).astype(_F32)


def _cast_kernel(w_ref, o_ref):
    o_ref[...] = w_ref[...].astype(o_ref.dtype)


def _to_bf16(w, layer):
    _, rows, cols = w.shape
    return pl.pallas_call(
        _cast_kernel,
        grid=(rows // CAST_ROWS,),
        in_specs=[pl.BlockSpec((None, CAST_ROWS, cols), lambda r: (layer, r, 0))],
        out_specs=pl.BlockSpec((CAST_ROWS, cols), lambda r: (r, 0)),
        out_shape=jax.ShapeDtypeStruct((rows, cols), _BF16),
        compiler_params=pltpu.CompilerParams(dimension_semantics=("arbitrary",)),
        name="weight_cast",
    )(w)


def kernel(x, a_w_in, a_ln_g, a_ln_b, a_w_s, a_b_s, a_w_out, b_w_in, b_w_pool, b_scale, b_w_out,
           c_w_in, c_conv_w, c_conv_b, c_ln_g, c_ln_b, c_w_out, post_ln_g, post_ln_b):
    e_tiles = D_INNER // PROJ_TN
    h_f = x.reshape(N_TOKENS, D_MODEL)
    h_b = h_f.astype(_BF16)
    for i in range(DEPTH):
        kind, j = i % N_MIXERS, i // N_MIXERS
        pl_g, pl_b = _row(post_ln_g[i]), _row(post_ln_b[i])
        if kind == 0:
            guz = _proj(h_b, a_w_in, j, [0, 2 * e_tiles], e_tiles,
                        lambda u, z: _gelu(u) * _silu(z), "sgu_in_uz")
            gv = _proj(h_b, a_w_in, j, [e_tiles], e_tiles, _gelu, "sgu_in_v")
            h_f, h_b = _sgu_out(guz, gv, h_f, _row(a_ln_g[j]), _row(a_ln_b[j]), a_w_s[j],
                                a_b_s[j].T, _to_bf16(a_w_out, j), pl_g, pl_b)
        elif kind == 1:
            v = _proj(h_b, b_w_in, j, [0], e_tiles, lambda a: a, "pool_in_v")
            sz = _proj(h_b, b_w_in, j, [e_tiles], e_tiles, _silu, "pool_in_z")
            w_pool = _to_bf16(b_w_pool.reshape(b_w_pool.shape[0], D_INNER, POOL_GROUP_DIM), j)
            h_f, h_b = _pool_out(v, sz, h_f, w_pool.reshape(b_w_pool.shape[1:]), _row(b_scale[j]),
                                 _to_bf16(b_w_out, j), pl_g, pl_b)
        else:
            w_out = _to_bf16(c_w_out, j)
            c = _glu_conv(h_b, c_w_in, j, c_conv_w[j], _row(c_conv_b[j]))
            sz = _proj(h_b, c_w_in, j, [2 * e_tiles], e_tiles, _silu, "conv_in_z")
            h_f, h_b = _conv_out(c, sz, h_f, _row(c_ln_g[j]), _row(c_ln_b[j]), w_out, pl_g, pl_b)
    return h_f.reshape(BATCH, SEQ, D_MODEL)
```

```python
import functools
import math

import jax
import jax.numpy as jnp
from jax import lax
from jax.experimental import pallas as pl
from jax.experimental.pallas import tpu as pltpu

D_MODEL = 2048
BATCH = 4
SEQ = 4096
DEPTH = 4
N_MIXERS = 3
D_INNER = 2 * D_MODEL
CHUNK = 128
SGU_HEADS = 8
SGU_HEAD_DIM = D_INNER // SGU_HEADS
POOL_WINDOWS = (2, 4, 8, 16)
POOL_GROUP_DIM = D_INNER // len(POOL_WINDOWS)
CONV_WIDTH = 31
LN_EPS = 1e-5
ALPHA = (2.0 * DEPTH) ** 0.25

N_TOKENS = BATCH * SEQ
POOL_HALO = 16
CONV_HALO = 32
LANES = 128
CONV_ROW_STRIDE = 4
CONV_ROWS = 64
GLU_TN = 512
GLU_SUB = 256

PROJ_TM = 1024
PROJ_TN = 1024
OUT_TM = 256
CAST_ROWS = 512
OUT_K_CHUNK = 512
VMEM_LIMIT = 56 * 1024 * 1024

_F32 = jnp.float32
_BF16 = jnp.bfloat16
_GELU_C = math.sqrt(2.0 / math.pi)


def _gelu(x):
    hx = 0.5 * x
    return hx + hx * jnp.tanh(x * (_GELU_C + (_GELU_C * 0.044715) * (x * x)))


def _gated(x, g):
    hx = 0.5 * x
    return hx + hx * jnp.tanh(0.5 * g)


def _silu(x):
    return _gated(x, x)


def _glu(a, g):
    return _gated(a, g)


def _layer_norm(x, g, b):
    mu = jnp.mean(x, axis=-1, keepdims=True)
    xc = x - mu
    var = jnp.mean(xc * xc, axis=-1, keepdims=True)
    return xc * lax.rsqrt(var + LN_EPS) * g + b


def _proj_kernel(x_ref, *refs, n_w, epilogue):
    w_refs, o_ref, wb_refs = refs[:n_w], refs[n_w], refs[n_w + 1:]

    @pl.when(pl.program_id(1) == 0)
    def _():
        for w_ref, wb_ref in zip(w_refs, wb_refs):
            wb_ref[...] = w_ref[...].astype(_BF16)

    x = x_ref[...]
    accs = [jnp.dot(x, wb_ref[...], preferred_element_type=_F32) for wb_ref in wb_refs]
    o_ref[...] = epilogue(*accs).astype(o_ref.dtype)


def _proj(x, w, layer, col_tile_offsets, n_col_tiles, epilogue, name):
    n, k = x.shape
    tm, tn = PROJ_TM, PROJ_TN
    n_w = len(col_tile_offsets)
    w_mode = pl.Buffered(1) if n_w > 1 else None
    in_specs = [pl.BlockSpec((tm, k), lambda j, i: (i, 0))]
    for off in col_tile_offsets:
        in_specs.append(pl.BlockSpec((None, k, tn), lambda j, i, off=off: (layer, 0, j + off),
                                     pipeline_mode=w_mode))
    return pl.pallas_call(
        functools.partial(_proj_kernel, n_w=n_w, epilogue=epilogue),
        grid=(n_col_tiles, n // tm),
        in_specs=in_specs,
        out_specs=pl.BlockSpec((tm, tn), lambda j, i: (i, j)),
        out_shape=jax.ShapeDtypeStruct((n, n_col_tiles * tn), _BF16),
        scratch_shapes=[pltpu.VMEM((k, tn), _BF16)] * n_w,
        compiler_params=pltpu.CompilerParams(
            dimension_semantics=("arbitrary", "arbitrary"), vmem_limit_bytes=VMEM_LIMIT),
        name=name,
    )(x, *([w] * n_w))


def _glu_conv_kernel(x_ref, wa_ref, wg_ref, cw_ref, cb_ref, o_ref, wab_ref, wgb_ref, ext_scr, stage_scr,
                     *, tiles_per_seq):
    i = pl.program_id(1)
    tm, tn = o_ref.shape
    q = CONV_ROWS // CONV_ROW_STRIDE

    @pl.when(i == 0)
    def _():
        wab_ref[...] = wa_ref[...].astype(_BF16)
        wgb_ref[...] = wg_ref[...].astype(_BF16)

    @pl.when(i % tiles_per_seq == 0)
    def _():
        ext_scr[:, 0:CONV_HALO, :] = jnp.zeros((tn // LANES, CONV_HALO, LANES), _F32)

    x = x_ref[...]
    lanes_per_sub = GLU_SUB // LANES
    for s in range(tn // GLU_SUB):
        cs = slice(s * GLU_SUB, (s + 1) * GLU_SUB)
        a = jnp.dot(x, wab_ref[:, cs], preferred_element_type=_F32)
        gl = jnp.dot(x, wgb_ref[:, cs], preferred_element_type=_F32)
        g = _glu(a, gl)
        for l2 in range(lanes_per_sub):
            l = s * lanes_per_sub + l2
            ls = slice(l * LANES, (l + 1) * LANES)
            ext_scr[l, CONV_HALO:, :] = g[:, l2 * LANES:(l2 + 1) * LANES]
            for r0 in range(0, tm, CONV_ROWS):
                accs = [jnp.broadcast_to(cb_ref[:, ls], (q, LANES))] * CONV_ROW_STRIDE
                for e in range(CONV_ROW_STRIDE - 1, -CONV_WIDTH, -1):
                    slab = ext_scr[l, pl.ds(CONV_HALO + r0 + e, q, stride=CONV_ROW_STRIDE), :]
                    for b in range(CONV_ROW_STRIDE):
                        d = b - e
                        if 0 <= d < CONV_WIDTH:
                            tap = CONV_WIDTH - 1 - d
                            accs[b] = accs[b] + slab * cw_ref[tap:tap + 1, ls]
                for b in range(CONV_ROW_STRIDE):
                    stage_scr[l, pl.ds(r0 + b, q, stride=CONV_ROW_STRIDE), :] = accs[b]
            ext_scr[l, 0:CONV_HALO, :] = ext_scr[l, tm:tm + CONV_HALO, :]
            o_ref[:, ls] = stage_scr[l].astype(o_ref.dtype)


def _glu_conv(x, w, layer, conv_w, conv_b):
    n, k = x.shape
    tm, tn = PROJ_TM, GLU_TN
    n_tiles = D_INNER // tn
    return pl.pallas_call(
        functools.partial(_glu_conv_kernel, tiles_per_seq=SEQ // tm),
        grid=(n_tiles, n // tm),
        in_specs=[pl.BlockSpec((tm, k), lambda j, i: (i, 0)),
                  pl.BlockSpec((None, k, tn), lambda j, i: (layer, 0, j)),
                  pl.BlockSpec((None, k, tn), lambda j, i: (layer, 0, j + n_tiles)),
                  pl.BlockSpec((CONV_WIDTH, tn), lambda j, i: (0, j)),
                  pl.BlockSpec((1, tn), lambda j, i: (0, j))],
        out_specs=pl.BlockSpec((tm, tn), lambda j, i: (i, j)),
        out_shape=jax.ShapeDtypeStruct((n, D_INNER), _BF16),
        scratch_shapes=[pltpu.VMEM((k, tn), _BF16), pltpu.VMEM((k, tn), _BF16),
                        pltpu.VMEM((tn // LANES, CONV_HALO + tm, LANES), _F32),
                        pltpu.VMEM((tn // LANES, tm, LANES), _F32)],
        compiler_params=pltpu.CompilerParams(
            dimension_semantics=("arbitrary", "arbitrary"), vmem_limit_bytes=VMEM_LIMIT),
        name="conv_in_glu",
    )(x, w, w, conv_w, conv_b)


def _out_proj_tail(y_scr, h_ref, wout_ref, plg_ref, plb_ref, of_ref, ob_ref):
    acc = jnp.dot(y_scr[...], wout_ref[...], preferred_element_type=_F32)
    o = _layer_norm(ALPHA * h_ref[...] + acc, plg_ref[...], plb_ref[...])
    of_ref[...] = o
    ob_ref[...] = o.astype(_BF16)


def _const_spec(shape):
    return pl.BlockSpec(shape, lambda i: (0,) * len(shape), pipeline_mode=pl.Buffered(1))


def _out_call(kernel, name, tile_inputs, const_inputs, scratch_shapes):
    tm = OUT_TM
    in_specs = [pl.BlockSpec(bs, im) for _, bs, im in tile_inputs]
    in_specs += [_const_spec(a.shape) for a in const_inputs]
    out_spec = pl.BlockSpec((tm, D_MODEL), lambda i: (i, 0))
    return pl.pallas_call(
        kernel,
        grid=(N_TOKENS // tm,),
        in_specs=in_specs,
        out_specs=[out_spec, out_spec],
        out_shape=[jax.ShapeDtypeStruct((N_TOKENS, D_MODEL), _F32),
                   jax.ShapeDtypeStruct((N_TOKENS, D_MODEL), _BF16)],
        scratch_shapes=scratch_shapes,
        compiler_params=pltpu.CompilerParams(
            dimension_semantics=("arbitrary",), vmem_limit_bytes=VMEM_LIMIT),
        name=name,
    )(*[a for a, _, _ in tile_inputs], *const_inputs)


def _halo_spec(rows):
    per_tile = OUT_TM // rows
    return (rows, D_INNER), (lambda i: (jnp.maximum(i * per_tile - 1, 0), 0))


def _sgu_out_kernel(guz_ref, gv_ref, h_ref, lng_ref, lnb_ref, ws_ref, bs_ref,
                    wout_ref, plg_ref, plb_ref, of_ref, ob_ref, vn_scr, y_scr):
    tm = guz_ref.shape[0]
    vn_scr[...] = _layer_norm(gv_ref[...].astype(_F32), lng_ref[...], lnb_ref[...]).astype(_BF16)
    row = lax.broadcasted_iota(jnp.int32, (CHUNK, CHUNK), 0)
    col = lax.broadcasted_iota(jnp.int32, (CHUNK, CHUNK), 1)
    causal = row >= col
    for hd in range(SGU_HEADS):
        w = jnp.where(causal, ws_ref[hd], 0.0).astype(_BF16)
        bias = bs_ref[:, hd:hd + 1]
        cs = slice(hd * SGU_HEAD_DIM, (hd + 1) * SGU_HEAD_DIM)
        for c in range(tm // CHUNK):
            rs = slice(c * CHUNK, (c + 1) * CHUNK)
            mixed = jnp.dot(w, vn_scr[rs, cs], preferred_element_type=_F32) + bias
            y_scr[rs, cs] = (guz_ref[rs, cs].astype(_F32) * mixed).astype(_BF16)
    _out_proj_tail(y_scr, h_ref, wout_ref, plg_ref, plb_ref, of_ref, ob_ref)


def _sgu_out(guz, gv, h_f, ln_g, ln_b, w_s, b_s_t, w_out, pl_g, pl_b):
    tm = OUT_TM
    tile = (tm, D_INNER)
    return _out_call(
        _sgu_out_kernel, "sgu_out",
        [(guz, tile, lambda i: (i, 0)), (gv, tile, lambda i: (i, 0)),
         (h_f, (tm, D_MODEL), lambda i: (i, 0))],
        [ln_g, ln_b, w_s, b_s_t, w_out, pl_g, pl_b],
        [pltpu.VMEM(tile, _BF16), pltpu.VMEM(tile, _BF16)])


def _pool_out_kernel(v_ref, vh_ref, sz_ref, h_ref, wpool_ref, scale_ref,
                     wout_ref, plg_ref, plb_ref, of_ref, ob_ref, y_scr):
    tm = v_ref.shape[0]
    pos0 = (pl.program_id(0) * tm) % SEQ
    seq_start = pos0 == 0
    gd = POOL_GROUP_DIM
    pos = pos0 + lax.broadcasted_iota(jnp.int32, (tm, gd), 0)
    for g, win in enumerate(POOL_WINDOWS):
        cs = slice(g * gd, (g + 1) * gd)
        halo = jnp.where(seq_start, 0.0, vh_ref[:, cs].astype(_F32))
        ext = jnp.concatenate([halo, v_ref[:, cs].astype(_F32)], axis=0)
        s, k = ext, 1
        while k < win:
            s = s + pltpu.roll(s, k, axis=0)
            k *= 2
        cnt = jnp.minimum(pos + 1, win).astype(_F32)
        p = s[POOL_HALO:] / cnt - ext[POOL_HALO:]
        q = jnp.dot(p.astype(_BF16), wpool_ref[g], preferred_element_type=_F32) * scale_ref[:, cs]
        y_scr[:, cs] = (q * sz_ref[:, cs].astype(_F32)).astype(_BF16)
    _out_proj_tail(y_scr, h_ref, wout_ref, plg_ref, plb_ref, of_ref, ob_ref)


def _pool_out(v, sz, h_f, w_pool, scale, w_out, pl_g, pl_b):
    tm = OUT_TM
    tile = (tm, D_INNER)
    halo_shape, halo_map = _halo_spec(POOL_HALO)
    return _out_call(
        _pool_out_kernel, "pool_out",
        [(v, tile, lambda i: (i, 0)), (v, halo_shape, halo_map), (sz, tile, lambda i: (i, 0)),
         (h_f, (tm, D_MODEL), lambda i: (i, 0))],
        [w_pool, scale, w_out, pl_g, pl_b],
        [pltpu.VMEM(tile, _BF16)])


def _conv_out_kernel(c_ref, sz_ref, h_ref, lng_ref, lnb_ref,
                     wout_ref, plg_ref, plb_ref, of_ref, ob_ref):
    c = c_ref[...].astype(_F32)
    mu = jnp.mean(c, axis=-1, keepdims=True)
    var = jnp.mean(jnp.square(c - mu), axis=-1, keepdims=True)
    rstd = lax.rsqrt(var + LN_EPS)
    acc = None
    for k in range(D_INNER // OUT_K_CHUNK):
        cs = slice(k * OUT_K_CHUNK, (k + 1) * OUT_K_CHUNK)
        s = _silu((c_ref[:, cs].astype(_F32) - mu) * rstd * lng_ref[:, cs] + lnb_ref[:, cs])
        y = (s * sz_ref[:, cs].astype(_F32)).astype(_BF16)
        part = jnp.dot(y, wout_ref[cs, :], preferred_element_type=_F32)
        acc = part if acc is None else acc + part
    o = _layer_norm(ALPHA * h_ref[...] + acc, plg_ref[...], plb_ref[...])
    of_ref[...] = o
    ob_ref[...] = o.astype(_BF16)


def _conv_out(c, sz, h_f, ln_g, ln_b, w_out, pl_g, pl_b):
    tm = OUT_TM
    tile = (tm, D_INNER)
    return _out_call(
        _conv_out_kernel, "conv_out",
        [(c, tile, lambda i: (i, 0)), (sz, tile, lambda i: (i, 0)), (h_f, (tm, D_MODEL), lambda i: (i, 0))],
        [ln_g, ln_b, w_out, pl_g, pl_b],
        [])


def _row(v):
    return v.reshape(1, -1).astype(_F32)


def _cast_kernel(w_ref, o_ref):
    o_ref[...] = w_ref[...].astype(o_ref.dtype)


def _to_bf16(w, layer):
    _, rows, cols = w.shape
    return pl.pallas_call(
        _cast_kernel,
        grid=(rows // CAST_ROWS,),
        in_specs=[pl.BlockSpec((None, CAST_ROWS, cols), lambda r: (layer, r, 0))],
        out_specs=pl.BlockSpec((CAST_ROWS, cols), lambda r: (r, 0)),
        out_shape=jax.ShapeDtypeStruct((rows, cols), _BF16),
        compiler_params=pltpu.CompilerParams(dimension_semantics=("arbitrary",)),
        name="weight_cast",
    )(w)


def kernel(x, a_w_in, a_ln_g, a_ln_b, a_w_s, a_b_s, a_w_out, b_w_in, b_w_pool, b_scale, b_w_out,
           c_w_in, c_conv_w, c_conv_b, c_ln_g, c_ln_b, c_w_out, post_ln_g, post_ln_b):
    e_tiles = D_INNER // PROJ_TN
    h_f = x.reshape(N_TOKENS, D_MODEL)
    h_b = h_f.astype(_BF16)
    for i in range(DEPTH):
        kind, j = i % N_MIXERS, i // N_MIXERS
        pl_g, pl_b = _row(post_ln_g[i]), _row(post_ln_b[i])
        if kind == 0:
            guz = _proj(h_b, a_w_in, j, [0, 2 * e_tiles], e_tiles,
                        lambda u, z: _gelu(u) * _silu(z), "sgu_in_uz")
            gv = _proj(h_b, a_w_in, j, [e_tiles], e_tiles, _gelu, "sgu_in_v")
            h_f, h_b = _sgu_out(guz, gv, h_f, _row(a_ln_g[j]), _row(a_ln_b[j]), a_w_s[j],
                                a_b_s[j].T, _to_bf16(a_w_out, j), pl_g, pl_b)
        elif kind == 1:
            v = _proj(h_b, b_w_in, j, [0], e_tiles, lambda a: a, "pool_in_v")
            sz = _proj(h_b, b_w_in, j, [e_tiles], e_tiles, _silu, "pool_in_z")
            w_pool = _to_bf16(b_w_pool.reshape(b_w_pool.shape[0], D_INNER, POOL_GROUP_DIM), j)
            h_f, h_b = _pool_out(v, sz, h_f, w_pool.reshape(b_w_pool.shape[1:]), _row(b_scale[j]),
                                 _to_bf16(b_w_out, j), pl_g, pl_b)
        else:
            w_out = _to_bf16(c_w_out, j)
            c = _glu_conv(h_b, c_w_in, j, c_conv_w[j], _row(c_conv_b[j]))
            sz = _proj(h_b, c_w_in, j, [2 * e_tiles], e_tiles, _silu, "conv_in_z")
            h_f, h_b = _conv_out(c, sz, h_f, _row(c_ln_g[j]), _row(c_ln_b[j]), w_out, pl_g, pl_b)
    return h_f.reshape(BATCH, SEQ, D_MODEL)
```

```python
import functools
import math

import jax
import jax.numpy as jnp
from jax import lax
from jax.experimental import pallas as pl
from jax.experimental.pallas import tpu as pltpu

D_MODEL = 2048
BATCH = 4
SEQ = 4096
DEPTH = 4
N_MIXERS = 3
D_INNER = 2 * D_MODEL
CHUNK = 128
SGU_HEADS = 8
SGU_HEAD_DIM = D_INNER // SGU_HEADS
POOL_WINDOWS = (2, 4, 8, 16)
POOL_GROUP_DIM = D_INNER // len(POOL_WINDOWS)
CONV_WIDTH = 31
LN_EPS = 1e-5
ALPHA = (2.0 * DEPTH) ** 0.25

N_TOKENS = BATCH * SEQ
POOL_HALO = 16
CONV_HALO = 32
LANES = 128
CONV_ROW_STRIDE = 4
CONV_ROWS = 64
GLU_TN = 512
GLU_SUB = 256

PROJ_TM = 1024
PROJ_TN = 1024
OUT_TM = 256
CAST_ROWS = 512
OUT_K_CHUNK = 512
SGU_HEADS_PER_DOT = 2
VMEM_LIMIT = 56 * 1024 * 1024

_F32 = jnp.float32
_BF16 = jnp.bfloat16
_GELU_C = math.sqrt(2.0 / math.pi)


def _gelu(x):
    hx = 0.5 * x
    return hx + hx * jnp.tanh(x * (_GELU_C + (_GELU_C * 0.044715) * (x * x)))


def _gated(x, g):
    hx = 0.5 * x
    return hx + hx * jnp.tanh(0.5 * g)


def _silu(x):
    return _gated(x, x)


def _glu(a, g):
    return _gated(a, g)


def _layer_norm(x, g, b):
    mu = jnp.mean(x, axis=-1, keepdims=True)
    xc = x - mu
    var = jnp.mean(xc * xc, axis=-1, keepdims=True)
    return xc * lax.rsqrt(var + LN_EPS) * g + b


def _proj_kernel(x_ref, *refs, n_w, epilogue):
    w_refs, o_ref, wb_refs = refs[:n_w], refs[n_w], refs[n_w + 1:]

    @pl.when(pl.program_id(1) == 0)
    def _():
        for w_ref, wb_ref in zip(w_refs, wb_refs):
            wb_ref[...] = w_ref[...].astype(_BF16)

    x = x_ref[...]
    accs = [jnp.dot(x, wb_ref[...], preferred_element_type=_F32) for wb_ref in wb_refs]
    o_ref[...] = epilogue(*accs).astype(o_ref.dtype)


def _proj(x, w, layer, col_tile_offsets, n_col_tiles, epilogue, name):
    n, k = x.shape
    tm, tn = PROJ_TM, PROJ_TN
    n_w = len(col_tile_offsets)
    w_mode = pl.Buffered(1) if n_w > 1 else None
    in_specs = [pl.BlockSpec((tm, k), lambda j, i: (i, 0))]
    for off in col_tile_offsets:
        in_specs.append(pl.BlockSpec((None, k, tn), lambda j, i, off=off: (layer, 0, j + off),
                                     pipeline_mode=w_mode))
    return pl.pallas_call(
        functools.partial(_proj_kernel, n_w=n_w, epilogue=epilogue),
        grid=(n_col_tiles, n // tm),
        in_specs=in_specs,
        out_specs=pl.BlockSpec((tm, tn), lambda j, i: (i, j)),
        out_shape=jax.ShapeDtypeStruct((n, n_col_tiles * tn), _BF16),
        scratch_shapes=[pltpu.VMEM((k, tn), _BF16)] * n_w,
        compiler_params=pltpu.CompilerParams(
            dimension_semantics=("arbitrary", "arbitrary"), vmem_limit_bytes=VMEM_LIMIT),
        name=name,
    )(x, *([w] * n_w))


def _glu_conv_kernel(x_ref, wa_ref, wg_ref, cw_ref, cb_ref, o_ref, wab_ref, wgb_ref, ext_scr, stage_scr,
                     *, tiles_per_seq):
    i = pl.program_id(1)
    tm, tn = o_ref.shape
    q = CONV_ROWS // CONV_ROW_STRIDE

    @pl.when(i == 0)
    def _():
        wab_ref[...] = wa_ref[...].astype(_BF16)
        wgb_ref[...] = wg_ref[...].astype(_BF16)

    @pl.when(i % tiles_per_seq == 0)
    def _():
        ext_scr[:, 0:CONV_HALO, :] = jnp.zeros((tn // LANES, CONV_HALO, LANES), _F32)

    x = x_ref[...]
    lanes_per_sub = GLU_SUB // LANES
    for s in range(tn // GLU_SUB):
        cs = slice(s * GLU_SUB, (s + 1) * GLU_SUB)
        a = jnp.dot(x, wab_ref[:, cs], preferred_element_type=_F32)
        gl = jnp.dot(x, wgb_ref[:, cs], preferred_element_type=_F32)
        g = _glu(a, gl)
        for l2 in range(lanes_per_sub):
            l = s * lanes_per_sub + l2
            ls = slice(l * LANES, (l + 1) * LANES)
            ext_scr[l, CONV_HALO:, :] = g[:, l2 * LANES:(l2 + 1) * LANES]
            for r0 in range(0, tm, CONV_ROWS):
                accs = [jnp.broadcast_to(cb_ref[:, ls], (q, LANES))] * CONV_ROW_STRIDE
                for e in range(CONV_ROW_STRIDE - 1, -CONV_WIDTH, -1):
                    slab = ext_scr[l, pl.ds(CONV_HALO + r0 + e, q, stride=CONV_ROW_STRIDE), :]
                    for b in range(CONV_ROW_STRIDE):
                        d = b - e
                        if 0 <= d < CONV_WIDTH:
                            tap = CONV_WIDTH - 1 - d
                            accs[b] = accs[b] + slab * cw_ref[tap:tap + 1, ls]
                for b in range(CONV_ROW_STRIDE):
                    stage_scr[l, pl.ds(r0 + b, q, stride=CONV_ROW_STRIDE), :] = accs[b]
            ext_scr[l, 0:CONV_HALO, :] = ext_scr[l, tm:tm + CONV_HALO, :]
            o_ref[:, ls] = stage_scr[l].astype(o_ref.dtype)


def _glu_conv(x, w, layer, conv_w, conv_b):
    n, k = x.shape
    tm, tn = PROJ_TM, GLU_TN
    n_tiles = D_INNER // tn
    return pl.pallas_call(
        functools.partial(_glu_conv_kernel, tiles_per_seq=SEQ // tm),
        grid=(n_tiles, n // tm),
        in_specs=[pl.BlockSpec((tm, k), lambda j, i: (i, 0)),
                  pl.BlockSpec((None, k, tn), lambda j, i: (layer, 0, j)),
                  pl.BlockSpec((None, k, tn), lambda j, i: (layer, 0, j + n_tiles)),
                  pl.BlockSpec((CONV_WIDTH, tn), lambda j, i: (0, j)),
                  pl.BlockSpec((1, tn), lambda j, i: (0, j))],
        out_specs=pl.BlockSpec((tm, tn), lambda j, i: (i, j)),
        out_shape=jax.ShapeDtypeStruct((n, D_INNER), _BF16),
        scratch_shapes=[pltpu.VMEM((k, tn), _BF16), pltpu.VMEM((k, tn), _BF16),
                        pltpu.VMEM((tn // LANES, CONV_HALO + tm, LANES), _F32),
                        pltpu.VMEM((tn // LANES, tm, LANES), _F32)],
        compiler_params=pltpu.CompilerParams(
            dimension_semantics=("arbitrary", "arbitrary"), vmem_limit_bytes=VMEM_LIMIT),
        name="conv_in_glu",
    )(x, w, w, conv_w, conv_b)


def _out_proj_tail(y_scr, h_ref, wout_ref, plg_ref, plb_ref, of_ref, ob_ref):
    acc = jnp.dot(y_scr[...], wout_ref[...], preferred_element_type=_F32)
    o = _layer_norm(ALPHA * h_ref[...] + acc, plg_ref[...], plb_ref[...])
    of_ref[...] = o
    ob_ref[...] = o.astype(_BF16)


def _const_spec(shape):
    return pl.BlockSpec(shape, lambda i: (0,) * len(shape), pipeline_mode=pl.Buffered(1))


def _out_call(kernel, name, tile_inputs, const_inputs, scratch_shapes):
    tm = OUT_TM
    in_specs = [pl.BlockSpec(bs, im) for _, bs, im in tile_inputs]
    in_specs += [_const_spec(a.shape) for a in const_inputs]
    out_spec = pl.BlockSpec((tm, D_MODEL), lambda i: (i, 0))
    return pl.pallas_call(
        kernel,
        grid=(N_TOKENS // tm,),
        in_specs=in_specs,
        out_specs=[out_spec, out_spec],
        out_shape=[jax.ShapeDtypeStruct((N_TOKENS, D_MODEL), _F32),
                   jax.ShapeDtypeStruct((N_TOKENS, D_MODEL), _BF16)],
        scratch_shapes=scratch_shapes,
        compiler_params=pltpu.CompilerParams(
            dimension_semantics=("arbitrary",), vmem_limit_bytes=VMEM_LIMIT),
        name=name,
    )(*[a for a, _, _ in tile_inputs], *const_inputs)


def _halo_spec(rows):
    per_tile = OUT_TM // rows
    return (rows, D_INNER), (lambda i: (jnp.maximum(i * per_tile - 1, 0), 0))


def _sgu_out_kernel(guz_ref, gv_ref, h_ref, lng_ref, lnb_ref, ws_ref, bs_ref,
                    wout_ref, plg_ref, plb_ref, of_ref, ob_ref, vn_scr, y_scr):
    tm = guz_ref.shape[0]
    vn_scr[...] = _layer_norm(gv_ref[...].astype(_F32), lng_ref[...], lnb_ref[...]).astype(_BF16)
    row = lax.broadcasted_iota(jnp.int32, (CHUNK, CHUNK), 0)
    col = lax.broadcasted_iota(jnp.int32, (CHUNK, CHUNK), 1)
    causal = row >= col
    acc = None
    for h0 in range(0, SGU_HEADS, SGU_HEADS_PER_DOT):
        for hd in range(h0, h0 + SGU_HEADS_PER_DOT):
            w = jnp.where(causal, ws_ref[hd], 0.0).astype(_BF16)
            bias = bs_ref[:, hd:hd + 1]
            cs = slice(hd * SGU_HEAD_DIM, (hd + 1) * SGU_HEAD_DIM)
            for c in range(tm // CHUNK):
                rs = slice(c * CHUNK, (c + 1) * CHUNK)
                mixed = jnp.dot(w, vn_scr[rs, cs], preferred_element_type=_F32) + bias
                y_scr[rs, cs] = (guz_ref[rs, cs].astype(_F32) * mixed).astype(_BF16)
        ks = slice(h0 * SGU_HEAD_DIM, (h0 + SGU_HEADS_PER_DOT) * SGU_HEAD_DIM)
        part = jnp.dot(y_scr[:, ks], wout_ref[ks, :], preferred_element_type=_F32)
        acc = part if acc is None else acc + part
    o = _layer_norm(ALPHA * h_ref[...] + acc, plg_ref[...], plb_ref[...])
    of_ref[...] = o
    ob_ref[...] = o.astype(_BF16)


def _sgu_out(guz, gv, h_f, ln_g, ln_b, w_s, b_s_t, w_out, pl_g, pl_b):
    tm = OUT_TM
    tile = (tm, D_INNER)
    return _out_call(
        _sgu_out_kernel, "sgu_out",
        [(guz, tile, lambda i: (i, 0)), (gv, tile, lambda i: (i, 0)),
         (h_f, (tm, D_MODEL), lambda i: (i, 0))],
        [ln_g, ln_b, w_s, b_s_t, w_out, pl_g, pl_b],
        [pltpu.VMEM(tile, _BF16), pltpu.VMEM(tile, _BF16)])


def _pool_out_kernel(v_ref, vh_ref, sz_ref, h_ref, wpool_ref, scale_ref,
                     wout_ref, plg_ref, plb_ref, of_ref, ob_ref, y_scr):
    tm = v_ref.shape[0]
    pos0 = (pl.program_id(0) * tm) % SEQ
    seq_start = pos0 == 0
    gd = POOL_GROUP_DIM
    pos = pos0 + lax.broadcasted_iota(jnp.int32, (tm, gd), 0)
    for g, win in enumerate(POOL_WINDOWS):
        cs = slice(g * gd, (g + 1) * gd)
        halo = jnp.where(seq_start, 0.0, vh_ref[:, cs].astype(_F32))
        ext = jnp.concatenate([halo, v_ref[:, cs].astype(_F32)], axis=0)
        s, k = ext, 1
        while k < win:
            s = s + pltpu.roll(s, k, axis=0)
            k *= 2
        cnt = jnp.minimum(pos + 1, win).astype(_F32)
        p = s[POOL_HALO:] / cnt - ext[POOL_HALO:]
        q = jnp.dot(p.astype(_BF16), wpool_ref[g], preferred_element_type=_F32) * scale_ref[:, cs]
        y_scr[:, cs] = (q * sz_ref[:, cs].astype(_F32)).astype(_BF16)
    _out_proj_tail(y_scr, h_ref, wout_ref, plg_ref, plb_ref, of_ref, ob_ref)


def _pool_out(v, sz, h_f, w_pool, scale, w_out, pl_g, pl_b):
    tm = OUT_TM
    tile = (tm, D_INNER)
    halo_shape, halo_map = _halo_spec(POOL_HALO)
    return _out_call(
        _pool_out_kernel, "pool_out",
        [(v, tile, lambda i: (i, 0)), (v, halo_shape, halo_map), (sz, tile, lambda i: (i, 0)),
         (h_f, (tm, D_MODEL), lambda i: (i, 0))],
        [w_pool, scale, w_out, pl_g, pl_b],
        [pltpu.VMEM(tile, _BF16)])


def _conv_out_kernel(c_ref, sz_ref, h_ref, lng_ref, lnb_ref,
                     wout_ref, plg_ref, plb_ref, of_ref, ob_ref):
    c = c_ref[...].astype(_F32)
    mu = jnp.mean(c, axis=-1, keepdims=True)
    var = jnp.mean(jnp.square(c - mu), axis=-1, keepdims=True)
    rstd = lax.rsqrt(var + LN_EPS)
    acc = None
    for k in range(D_INNER // OUT_K_CHUNK):
        cs = slice(k * OUT_K_CHUNK, (k + 1) * OUT_K_CHUNK)
        s = _silu((c_ref[:, cs].astype(_F32) - mu) * rstd * lng_ref[:, cs] + lnb_ref[:, cs])
        y = (s * sz_ref[:, cs].astype(_F32)).astype(_BF16)
        part = jnp.dot(y, wout_ref[cs, :], preferred_element_type=_F32)
        acc = part if acc is None else acc + part
    o = _layer_norm(ALPHA * h_ref[...] + acc, plg_ref[...], plb_ref[...])
    of_ref[...] = o
    ob_ref[...] = o.astype(_BF16)


def _conv_out(c, sz, h_f, ln_g, ln_b, w_out, pl_g, pl_b):
    tm = OUT_TM
    tile = (tm, D_INNER)
    return _out_call(
        _conv_out_kernel, "conv_out",
        [(c, tile, lambda i: (i, 0)), (sz, tile, lambda i: (i, 0)), (h_f, (tm, D_MODEL), lambda i: (i, 0))],
        [ln_g, ln_b, w_out, pl_g, pl_b],
        [])


def _row(v):
    return v.reshape(1, -1).astype(_F32)


def _cast_kernel(w_ref, o_ref):
    o_ref[...] = w_ref[...].astype(o_ref.dtype)


def _to_bf16(w, layer):
    _, rows, cols = w.shape
    return pl.pallas_call(
        _cast_kernel,
        grid=(rows // CAST_ROWS,),
        in_specs=[pl.BlockSpec((None, CAST_ROWS, cols), lambda r: (layer, r, 0))],
        out_specs=pl.BlockSpec((CAST_ROWS, cols), lambda r: (r, 0)),
        out_shape=jax.ShapeDtypeStruct((rows, cols), _BF16),
        compiler_params=pltpu.CompilerParams(dimension_semantics=("arbitrary",)),
        name="weight_cast",
    )(w)


def kernel(x, a_w_in, a_ln_g, a_ln_b, a_w_s, a_b_s, a_w_out, b_w_in, b_w_pool, b_scale, b_w_out,
           c_w_in, c_conv_w, c_conv_b, c_ln_g, c_ln_b, c_w_out, post_ln_g, post_ln_b):
    e_tiles = D_INNER // PROJ_TN
    h_f = x.reshape(N_TOKENS, D_MODEL)
    h_b = h_f.astype(_BF16)
    for i in range(DEPTH):
        kind, j = i % N_MIXERS, i // N_MIXERS
        pl_g, pl_b = _row(post_ln_g[i]), _row(post_ln_b[i])
        if kind == 0:
            guz = _proj(h_b, a_w_in, j, [0, 2 * e_tiles], e_tiles,
                        lambda u, z: _gelu(u) * _silu(z), "sgu_in_uz")
            gv = _proj(h_b, a_w_in, j, [e_tiles], e_tiles, _gelu, "sgu_in_v")
            h_f, h_b = _sgu_out(guz, gv, h_f, _row(a_ln_g[j]), _row(a_ln_b[j]), a_w_s[j],
                                a_b_s[j].T, _to_bf16(a_w_out, j), pl_g, pl_b)
        elif kind == 1:
            v = _proj(h_b, b_w_in, j, [0], e_tiles, lambda a: a, "pool_in_v")
            sz = _proj(h_b, b_w_in, j, [e_tiles], e_tiles, _silu, "pool_in_z")
            w_pool = _to_bf16(b_w_pool.reshape(b_w_pool.shape[0], D_INNER, POOL_GROUP_DIM), j)
            h_f, h_b = _pool_out(v, sz, h_f, w_pool.reshape(b_w_pool.shape[1:]), _row(b_scale[j]),
                                 _to_bf16(b_w_out, j), pl_g, pl_b)
        else:
            w_out = _to_bf16(c_w_out, j)
            c = _glu_conv(h_b, c_w_in, j, c_conv_w[j], _row(c_conv_b[j]))
            sz = _proj(h_b, c_w_in, j, [2 * e_tiles], e_tiles, _silu, "conv_in_z")
            h_f, h_b = _conv_out(c, sz, h_f, _row(c_ln_g[j]), _row(c_ln_b[j]), w_out, pl_g, pl_b)
    return h_f.reshape(BATCH, SEQ, D_MODEL)
```

```python
import functools
import math

import jax
import jax.numpy as jnp
from jax import lax
from jax.experimental import pallas as pl
from jax.experimental.pallas import tpu as pltpu

D_MODEL = 2048
BATCH = 4
SEQ = 4096
DEPTH = 4
N_MIXERS = 3
D_INNER = 2 * D_MODEL
CHUNK = 128
SGU_HEADS = 8
SGU_HEAD_DIM = D_INNER // SGU_HEADS
POOL_WINDOWS = (2, 4, 8, 16)
POOL_GROUP_DIM = D_INNER // len(POOL_WINDOWS)
CONV_WIDTH = 31
LN_EPS = 1e-5
ALPHA = (2.0 * DEPTH) ** 0.25

N_TOKENS = BATCH * SEQ
POOL_HALO = 16
CONV_HALO = 32
LANES = 128
CONV_ROW_STRIDE = 4
CONV_ROWS = 64
GLU_TN = 512
GLU_SUB = 256

PROJ_TM = 1024
PROJ_TN = 1024
OUT_TM = 256
OUT_K_CHUNK = 512
SGU_HEADS_PER_DOT = 2
VMEM_LIMIT = 56 * 1024 * 1024

_F32 = jnp.float32
_BF16 = jnp.bfloat16
_GELU_C = math.sqrt(2.0 / math.pi)


def _gelu(x):
    hx = 0.5 * x
    return hx + hx * jnp.tanh(x * (_GELU_C + (_GELU_C * 0.044715) * (x * x)))


def _gated(x, g):
    hx = 0.5 * x
    return hx + hx * jnp.tanh(0.5 * g)


def _silu(x):
    return _gated(x, x)


def _glu(a, g):
    return _gated(a, g)


def _layer_norm(x, g, b):
    mu = jnp.mean(x, axis=-1, keepdims=True)
    xc = x - mu
    var = jnp.mean(xc * xc, axis=-1, keepdims=True)
    return xc * lax.rsqrt(var + LN_EPS) * g + b


def _proj_kernel(x_ref, *refs, n_w, epilogue, rider):
    w_refs, refs = refs[:n_w], refs[n_w:]
    if rider:
        r_ref, o_ref, ro_ref = refs[:3]
        wb_refs = refs[3:]
        ro_ref[...] = r_ref[...].astype(_BF16)
    else:
        o_ref, wb_refs = refs[0], refs[1:]

    @pl.when(pl.program_id(1) == 0)
    def _():
        for w_ref, wb_ref in zip(w_refs, wb_refs):
            wb_ref[...] = w_ref[...].astype(_BF16)

    x = x_ref[...]
    accs = [jnp.dot(x, wb_ref[...], preferred_element_type=_F32) for wb_ref in wb_refs]
    o_ref[...] = epilogue(*accs).astype(o_ref.dtype)


def _proj(x, w, layer, col_tile_offsets, n_col_tiles, epilogue, name, cast=None):
    n, k = x.shape
    tm, tn = PROJ_TM, PROJ_TN
    n_w = len(col_tile_offsets)
    n_i = n // tm
    w_mode = pl.Buffered(1) if n_w > 1 else None
    in_specs = [pl.BlockSpec((tm, k), lambda j, i: (i, 0))]
    for off in col_tile_offsets:
        in_specs.append(pl.BlockSpec((None, k, tn), lambda j, i, off=off: (layer, 0, j + off),
                                     pipeline_mode=w_mode))
    out_specs = [pl.BlockSpec((tm, tn), lambda j, i: (i, j))]
    out_shape = [jax.ShapeDtypeStruct((n, n_col_tiles * tn), _BF16)]
    operands = [x] + [w] * n_w
    if cast is not None:
        src, src_layer = cast
        _, rows, cols = src.shape
        slab = rows // (n_col_tiles * n_i)
        in_specs.append(pl.BlockSpec((None, slab, cols), lambda j, i: (src_layer, j * n_i + i, 0)))
        out_specs.append(pl.BlockSpec((slab, cols), lambda j, i: (j * n_i + i, 0)))
        out_shape.append(jax.ShapeDtypeStruct((rows, cols), _BF16))
        operands.append(src)
    outs = pl.pallas_call(
        functools.partial(_proj_kernel, n_w=n_w, epilogue=epilogue, rider=cast is not None),
        grid=(n_col_tiles, n_i),
        in_specs=in_specs,
        out_specs=out_specs,
        out_shape=out_shape,
        scratch_shapes=[pltpu.VMEM((k, tn), _BF16)] * n_w,
        compiler_params=pltpu.CompilerParams(
            dimension_semantics=("arbitrary", "arbitrary"), vmem_limit_bytes=VMEM_LIMIT),
        name=name,
    )(*operands)
    return outs if cast is not None else outs[0]


def _glu_conv_kernel(x_ref, wa_ref, wg_ref, cw_ref, cb_ref, o_ref, wab_ref, wgb_ref, ext_scr, stage_scr,
                     *, tiles_per_seq):
    i = pl.program_id(1)
    tm, tn = o_ref.shape
    q = CONV_ROWS // CONV_ROW_STRIDE

    @pl.when(i == 0)
    def _():
        wab_ref[...] = wa_ref[...].astype(_BF16)
        wgb_ref[...] = wg_ref[...].astype(_BF16)

    @pl.when(i % tiles_per_seq == 0)
    def _():
        ext_scr[:, 0:CONV_HALO, :] = jnp.zeros((tn // LANES, CONV_HALO, LANES), _F32)

    x = x_ref[...]
    lanes_per_sub = GLU_SUB // LANES
    for s in range(tn // GLU_SUB):
        cs = slice(s * GLU_SUB, (s + 1) * GLU_SUB)
        a = jnp.dot(x, wab_ref[:, cs], preferred_element_type=_F32)
        gl = jnp.dot(x, wgb_ref[:, cs], preferred_element_type=_F32)
        g = _glu(a, gl)
        for l2 in range(lanes_per_sub):
            l = s * lanes_per_sub + l2
            ls = slice(l * LANES, (l + 1) * LANES)
            ext_scr[l, CONV_HALO:, :] = g[:, l2 * LANES:(l2 + 1) * LANES]
            for r0 in range(0, tm, CONV_ROWS):
                accs = [jnp.broadcast_to(cb_ref[:, ls], (q, LANES))] * CONV_ROW_STRIDE
                for e in range(CONV_ROW_STRIDE - 1, -CONV_WIDTH, -1):
                    slab = ext_scr[l, pl.ds(CONV_HALO + r0 + e, q, stride=CONV_ROW_STRIDE), :]
                    for b in range(CONV_ROW_STRIDE):
                        d = b - e
                        if 0 <= d < CONV_WIDTH:
                            tap = CONV_WIDTH - 1 - d
                            accs[b] = accs[b] + slab * cw_ref[tap:tap + 1, ls]
                for b in range(CONV_ROW_STRIDE):
                    stage_scr[l, pl.ds(r0 + b, q, stride=CONV_ROW_STRIDE), :] = accs[b]
            ext_scr[l, 0:CONV_HALO, :] = ext_scr[l, tm:tm + CONV_HALO, :]
            o_ref[:, ls] = stage_scr[l].astype(o_ref.dtype)


def _glu_conv(x, w, layer, conv_w, conv_b):
    n, k = x.shape
    tm, tn = PROJ_TM, GLU_TN
    n_tiles = D_INNER // tn
    return pl.pallas_call(
        functools.partial(_glu_conv_kernel, tiles_per_seq=SEQ // tm),
        grid=(n_tiles, n // tm),
        in_specs=[pl.BlockSpec((tm, k), lambda j, i: (i, 0)),
                  pl.BlockSpec((None, k, tn), lambda j, i: (layer, 0, j)),
                  pl.BlockSpec((None, k, tn), lambda j, i: (layer, 0, j + n_tiles)),
                  pl.BlockSpec((CONV_WIDTH, tn), lambda j, i: (0, j)),
                  pl.BlockSpec((1, tn), lambda j, i: (0, j))],
        out_specs=pl.BlockSpec((tm, tn), lambda j, i: (i, j)),
        out_shape=jax.ShapeDtypeStruct((n, D_INNER), _BF16),
        scratch_shapes=[pltpu.VMEM((k, tn), _BF16), pltpu.VMEM((k, tn), _BF16),
                        pltpu.VMEM((tn // LANES, CONV_HALO + tm, LANES), _F32),
                        pltpu.VMEM((tn // LANES, tm, LANES), _F32)],
        compiler_params=pltpu.CompilerParams(
            dimension_semantics=("arbitrary", "arbitrary"), vmem_limit_bytes=VMEM_LIMIT),
        name="conv_in_glu",
    )(x, w, w, conv_w, conv_b)


def _out_proj_tail(y_scr, h_ref, wout_ref, plg_ref, plb_ref, of_ref, ob_ref):
    acc = jnp.dot(y_scr[...], wout_ref[...], preferred_element_type=_F32)
    o = _layer_norm(ALPHA * h_ref[...] + acc, plg_ref[...], plb_ref[...])
    of_ref[...] = o
    ob_ref[...] = o.astype(_BF16)


def _const_spec(shape):
    return pl.BlockSpec(shape, lambda i: (0,) * len(shape), pipeline_mode=pl.Buffered(1))


def _out_call(kernel, name, tile_inputs, const_inputs, scratch_shapes):
    tm = OUT_TM
    in_specs = [pl.BlockSpec(bs, im) for _, bs, im in tile_inputs]
    in_specs += [_const_spec(a.shape) for a in const_inputs]
    out_spec = pl.BlockSpec((tm, D_MODEL), lambda i: (i, 0))
    return pl.pallas_call(
        kernel,
        grid=(N_TOKENS // tm,),
        in_specs=in_specs,
        out_specs=[out_spec, out_spec],
        out_shape=[jax.ShapeDtypeStruct((N_TOKENS, D_MODEL), _F32),
                   jax.ShapeDtypeStruct((N_TOKENS, D_MODEL), _BF16)],
        scratch_shapes=scratch_shapes,
        compiler_params=pltpu.CompilerParams(
            dimension_semantics=("arbitrary",), vmem_limit_bytes=VMEM_LIMIT),
        name=name,
    )(*[a for a, _, _ in tile_inputs], *const_inputs)


def _halo_spec(rows):
    per_tile = OUT_TM // rows
    return (rows, D_INNER), (lambda i: (jnp.maximum(i * per_tile - 1, 0), 0))


def _sgu_out_kernel(guz_ref, gv_ref, h_ref, lng_ref, lnb_ref, ws_ref, bs_ref,
                    wout_ref, plg_ref, plb_ref, of_ref, ob_ref, vn_scr, y_scr):
    tm = guz_ref.shape[0]
    vn_scr[...] = _layer_norm(gv_ref[...].astype(_F32), lng_ref[...], lnb_ref[...]).astype(_BF16)
    row = lax.broadcasted_iota(jnp.int32, (CHUNK, CHUNK), 0)
    col = lax.broadcasted_iota(jnp.int32, (CHUNK, CHUNK), 1)
    causal = row >= col
    acc = None
    for h0 in range(0, SGU_HEADS, SGU_HEADS_PER_DOT):
        for hd in range(h0, h0 + SGU_HEADS_PER_DOT):
            w = jnp.where(causal, ws_ref[hd], 0.0).astype(_BF16)
            bias = bs_ref[:, hd:hd + 1]
            cs = slice(hd * SGU_HEAD_DIM, (hd + 1) * SGU_HEAD_DIM)
            for c in range(tm // CHUNK):
                rs = slice(c * CHUNK, (c + 1) * CHUNK)
                mixed = jnp.dot(w, vn_scr[rs, cs], preferred_element_type=_F32) + bias
                y_scr[rs, cs] = (guz_ref[rs, cs].astype(_F32) * mixed).astype(_BF16)
        ks = slice(h0 * SGU_HEAD_DIM, (h0 + SGU_HEADS_PER_DOT) * SGU_HEAD_DIM)
        part = jnp.dot(y_scr[:, ks], wout_ref[ks, :], preferred_element_type=_F32)
        acc = part if acc is None else acc + part
    o = _layer_norm(ALPHA * h_ref[...] + acc, plg_ref[...], plb_ref[...])
    of_ref[...] = o
    ob_ref[...] = o.astype(_BF16)


def _sgu_out(guz, gv, h_f, ln_g, ln_b, w_s, b_s_t, w_out, pl_g, pl_b):
    tm = OUT_TM
    tile = (tm, D_INNER)
    return _out_call(
        _sgu_out_kernel, "sgu_out",
        [(guz, tile, lambda i: (i, 0)), (gv, tile, lambda i: (i, 0)),
         (h_f, (tm, D_MODEL), lambda i: (i, 0))],
        [ln_g, ln_b, w_s, b_s_t, w_out, pl_g, pl_b],
        [pltpu.VMEM(tile, _BF16), pltpu.VMEM(tile, _BF16)])


def _pool_out_kernel(v_ref, vh_ref, sz_ref, h_ref, wpool_ref, scale_ref,
                     wout_ref, plg_ref, plb_ref, of_ref, ob_ref, y_scr):
    tm = v_ref.shape[0]
    pos0 = (pl.program_id(0) * tm) % SEQ
    seq_start = pos0 == 0
    gd = POOL_GROUP_DIM
    pos = pos0 + lax.broadcasted_iota(jnp.int32, (tm, gd), 0)
    for g, win in enumerate(POOL_WINDOWS):
        cs = slice(g * gd, (g + 1) * gd)
        halo = jnp.where(seq_start, 0.0, vh_ref[:, cs].astype(_F32))
        ext = jnp.concatenate([halo, v_ref[:, cs].astype(_F32)], axis=0)
        s, k = ext, 1
        while k < win:
            s = s + pltpu.roll(s, k, axis=0)
            k *= 2
        cnt = jnp.minimum(pos + 1, win).astype(_F32)
        p = s[POOL_HALO:] / cnt - ext[POOL_HALO:]
        q = jnp.dot(p.astype(_BF16), wpool_ref[g], preferred_element_type=_F32) * scale_ref[:, cs]
        y_scr[:, cs] = (q * sz_ref[:, cs].astype(_F32)).astype(_BF16)
    _out_proj_tail(y_scr, h_ref, wout_ref, plg_ref, plb_ref, of_ref, ob_ref)


def _pool_out(v, sz, h_f, w_pool, scale, w_out, pl_g, pl_b):
    tm = OUT_TM
    tile = (tm, D_INNER)
    halo_shape, halo_map = _halo_spec(POOL_HALO)
    return _out_call(
        _pool_out_kernel, "pool_out",
        [(v, tile, lambda i: (i, 0)), (v, halo_shape, halo_map), (sz, tile, lambda i: (i, 0)),
         (h_f, (tm, D_MODEL), lambda i: (i, 0))],
        [w_pool, scale, w_out, pl_g, pl_b],
        [pltpu.VMEM(tile, _BF16)])


def _conv_out_kernel(c_ref, sz_ref, h_ref, lng_ref, lnb_ref,
                     wout_ref, plg_ref, plb_ref, of_ref, ob_ref):
    c = c_ref[...].astype(_F32)
    mu = jnp.mean(c, axis=-1, keepdims=True)
    var = jnp.mean(jnp.square(c - mu), axis=-1, keepdims=True)
    rstd = lax.rsqrt(var + LN_EPS)
    acc = None
    for k in range(D_INNER // OUT_K_CHUNK):
        cs = slice(k * OUT_K_CHUNK, (k + 1) * OUT_K_CHUNK)
        s = _silu((c_ref[:, cs].astype(_F32) - mu) * rstd * lng_ref[:, cs] + lnb_ref[:, cs])
        y = (s * sz_ref[:, cs].astype(_F32)).astype(_BF16)
        part = jnp.dot(y, wout_ref[cs, :], preferred_element_type=_F32)
        acc = part if acc is None else acc + part
    o = _layer_norm(ALPHA * h_ref[...] + acc, plg_ref[...], plb_ref[...])
    of_ref[...] = o
    ob_ref[...] = o.astype(_BF16)


def _conv_out(c, sz, h_f, ln_g, ln_b, w_out, pl_g, pl_b):
    tm = OUT_TM
    tile = (tm, D_INNER)
    return _out_call(
        _conv_out_kernel, "conv_out",
        [(c, tile, lambda i: (i, 0)), (sz, tile, lambda i: (i, 0)), (h_f, (tm, D_MODEL), lambda i: (i, 0))],
        [ln_g, ln_b, w_out, pl_g, pl_b],
        [])


def _row(v):
    return v.reshape(1, -1).astype(_F32)


def kernel(x, a_w_in, a_ln_g, a_ln_b, a_w_s, a_b_s, a_w_out, b_w_in, b_w_pool, b_scale, b_w_out,
           c_w_in, c_conv_w, c_conv_b, c_ln_g, c_ln_b, c_w_out, post_ln_g, post_ln_b):
    e_tiles = D_INNER // PROJ_TN
    h_f = x.reshape(N_TOKENS, D_MODEL)
    h_b = h_f.astype(_BF16)
    for i in range(DEPTH):
        kind, j = i % N_MIXERS, i // N_MIXERS
        pl_g, pl_b = _row(post_ln_g[i]), _row(post_ln_b[i])
        if kind == 0:
            guz = _proj(h_b, a_w_in, j, [0, 2 * e_tiles], e_tiles,
                        lambda u, z: _gelu(u) * _silu(z), "sgu_in_uz")
            gv, w_out = _proj(h_b, a_w_in, j, [e_tiles], e_tiles, _gelu, "sgu_in_v", cast=(a_w_out, j))
            h_f, h_b = _sgu_out(guz, gv, h_f, _row(a_ln_g[j]), _row(a_ln_b[j]), a_w_s[j],
                                a_b_s[j].T, w_out, pl_g, pl_b)
        elif kind == 1:
            w_pool_rows = b_w_pool.reshape(b_w_pool.shape[0], D_INNER, POOL_GROUP_DIM)
            v, w_pool = _proj(h_b, b_w_in, j, [0], e_tiles, lambda a: a, "pool_in_v",
                              cast=(w_pool_rows, j))
            sz, w_out = _proj(h_b, b_w_in, j, [e_tiles], e_tiles, _silu, "pool_in_z", cast=(b_w_out, j))
            h_f, h_b = _pool_out(v, sz, h_f, w_pool.reshape(b_w_pool.shape[1:]), _row(b_scale[j]),
                                 w_out, pl_g, pl_b)
        else:
            c = _glu_conv(h_b, c_w_in, j, c_conv_w[j], _row(c_conv_b[j]))
            sz, w_out = _proj(h_b, c_w_in, j, [2 * e_tiles], e_tiles, _silu, "conv_in_z", cast=(c_w_out, j))
            h_f, h_b = _conv_out(c, sz, h_f, _row(c_ln_g[j]), _row(c_ln_b[j]), w_out, pl_g, pl_b)
    return h_f.reshape(BATCH, SEQ, D_MODEL)
```

```python
import functools
import math

import jax
import jax.numpy as jnp
from jax import lax
from jax.experimental import pallas as pl
from jax.experimental.pallas import tpu as pltpu

D_MODEL = 2048
BATCH = 4
SEQ = 4096
DEPTH = 4
N_MIXERS = 3
D_INNER = 2 * D_MODEL
CHUNK = 128
SGU_HEADS = 8
SGU_HEAD_DIM = D_INNER // SGU_HEADS
POOL_WINDOWS = (2, 4, 8, 16)
POOL_GROUP_DIM = D_INNER // len(POOL_WINDOWS)
CONV_WIDTH = 31
LN_EPS = 1e-5
ALPHA = (2.0 * DEPTH) ** 0.25

N_TOKENS = BATCH * SEQ
POOL_HALO = 16
CONV_HALO = 32
LANES = 128
CONV_ROW_STRIDE = 4
CONV_ROWS = 64
GLU_TN = 512
GLU_SUB = 256

PROJ_TM = 1024
PROJ_TN = 1024
OUT_TM = 256
OUT_K_CHUNK = 512
SGU_HEADS_PER_DOT = 2
VMEM_LIMIT = 56 * 1024 * 1024

_F32 = jnp.float32
_BF16 = jnp.bfloat16
_GELU_C = math.sqrt(2.0 / math.pi)


def _gelu(x):
    hx = 0.5 * x
    return hx + hx * jnp.tanh(x * (_GELU_C + (_GELU_C * 0.044715) * (x * x)))


def _gated(x, g):
    hx = 0.5 * x
    return hx + hx * jnp.tanh(0.5 * g)


def _silu(x):
    return _gated(x, x)


def _glu(a, g):
    return _gated(a, g)


def _layer_norm(x, g, b):
    mu = jnp.mean(x, axis=-1, keepdims=True)
    xc = x - mu
    var = jnp.mean(xc * xc, axis=-1, keepdims=True)
    return xc * lax.rsqrt(var + LN_EPS) * g + b


def _proj_kernel(x_ref, *refs, n_w, epilogue, rider):
    w_refs, refs = refs[:n_w], refs[n_w:]
    if rider:
        r_ref, o_ref, ro_ref = refs[:3]
        wb_refs = refs[3:]
        ro_ref[...] = r_ref[...].astype(_BF16)
    else:
        o_ref, wb_refs = refs[0], refs[1:]

    @pl.when(pl.program_id(1) == 0)
    def _():
        for w_ref, wb_ref in zip(w_refs, wb_refs):
            wb_ref[...] = w_ref[...].astype(_BF16)

    for m in range(0, x_ref.shape[0], PROJ_TM):
        x = x_ref[m:m + PROJ_TM, :]
        accs = [jnp.dot(x, wb_ref[...], preferred_element_type=_F32) for wb_ref in wb_refs]
        o_ref[m:m + PROJ_TM, :] = epilogue(*accs).astype(o_ref.dtype)


def _proj(x, w, layer, col_tile_offsets, n_col_tiles, epilogue, name, cast=None):
    n, k = x.shape
    n_w = len(col_tile_offsets)
    tm, tn = (PROJ_TM if n_w > 1 else 2 * PROJ_TM), PROJ_TN
    n_i = n // tm
    w_mode = pl.Buffered(1) if n_w > 1 else None
    in_specs = [pl.BlockSpec((tm, k), lambda j, i: (i, 0))]
    for off in col_tile_offsets:
        in_specs.append(pl.BlockSpec((None, k, tn), lambda j, i, off=off: (layer, 0, j + off),
                                     pipeline_mode=w_mode))
    out_specs = [pl.BlockSpec((tm, tn), lambda j, i: (i, j))]
    out_shape = [jax.ShapeDtypeStruct((n, n_col_tiles * tn), _BF16)]
    operands = [x] + [w] * n_w
    if cast is not None:
        src, src_layer = cast
        _, rows, cols = src.shape
        slab = rows // (n_col_tiles * n_i)
        in_specs.append(pl.BlockSpec((None, slab, cols), lambda j, i: (src_layer, j * n_i + i, 0)))
        out_specs.append(pl.BlockSpec((slab, cols), lambda j, i: (j * n_i + i, 0)))
        out_shape.append(jax.ShapeDtypeStruct((rows, cols), _BF16))
        operands.append(src)
    outs = pl.pallas_call(
        functools.partial(_proj_kernel, n_w=n_w, epilogue=epilogue, rider=cast is not None),
        grid=(n_col_tiles, n_i),
        in_specs=in_specs,
        out_specs=out_specs,
        out_shape=out_shape,
        scratch_shapes=[pltpu.VMEM((k, tn), _BF16)] * n_w,
        compiler_params=pltpu.CompilerParams(
            dimension_semantics=("arbitrary", "arbitrary"), vmem_limit_bytes=VMEM_LIMIT),
        name=name,
    )(*operands)
    return outs if cast is not None else outs[0]


def _glu_conv_kernel(x_ref, wa_ref, wg_ref, cw_ref, cb_ref, o_ref, wab_ref, wgb_ref, ext_scr, stage_scr,
                     *, tiles_per_seq):
    i = pl.program_id(1)
    tm, tn = o_ref.shape
    q = CONV_ROWS // CONV_ROW_STRIDE

    @pl.when(i == 0)
    def _():
        wab_ref[...] = wa_ref[...].astype(_BF16)
        wgb_ref[...] = wg_ref[...].astype(_BF16)

    @pl.when(i % tiles_per_seq == 0)
    def _():
        ext_scr[:, 0:CONV_HALO, :] = jnp.zeros((tn // LANES, CONV_HALO, LANES), _F32)

    x = x_ref[...]
    lanes_per_sub = GLU_SUB // LANES
    for s in range(tn // GLU_SUB):
        cs = slice(s * GLU_SUB, (s + 1) * GLU_SUB)
        a = jnp.dot(x, wab_ref[:, cs], preferred_element_type=_F32)
        gl = jnp.dot(x, wgb_ref[:, cs], preferred_element_type=_F32)
        g = _glu(a, gl)
        for l2 in range(lanes_per_sub):
            l = s * lanes_per_sub + l2
            ls = slice(l * LANES, (l + 1) * LANES)
            ext_scr[l, CONV_HALO:, :] = g[:, l2 * LANES:(l2 + 1) * LANES]
            for r0 in range(0, tm, CONV_ROWS):
                accs = [jnp.broadcast_to(cb_ref[:, ls], (q, LANES))] * CONV_ROW_STRIDE
                for e in range(CONV_ROW_STRIDE - 1, -CONV_WIDTH, -1):
                    slab = ext_scr[l, pl.ds(CONV_HALO + r0 + e, q, stride=CONV_ROW_STRIDE), :]
                    for b in range(CONV_ROW_STRIDE):
                        d = b - e
                        if 0 <= d < CONV_WIDTH:
                            tap = CONV_WIDTH - 1 - d
                            accs[b] = accs[b] + slab * cw_ref[tap:tap + 1, ls]
                for b in range(CONV_ROW_STRIDE):
                    stage_scr[l, pl.ds(r0 + b, q, stride=CONV_ROW_STRIDE), :] = accs[b]
            ext_scr[l, 0:CONV_HALO, :] = ext_scr[l, tm:tm + CONV_HALO, :]
            o_ref[:, ls] = stage_scr[l].astype(o_ref.dtype)


def _glu_conv(x, w, layer, conv_w, conv_b):
    n, k = x.shape
    tm, tn = PROJ_TM, GLU_TN
    n_tiles = D_INNER // tn
    return pl.pallas_call(
        functools.partial(_glu_conv_kernel, tiles_per_seq=SEQ // tm),
        grid=(n_tiles, n // tm),
        in_specs=[pl.BlockSpec((tm, k), lambda j, i: (i, 0)),
                  pl.BlockSpec((None, k, tn), lambda j, i: (layer, 0, j)),
                  pl.BlockSpec((None, k, tn), lambda j, i: (layer, 0, j + n_tiles)),
                  pl.BlockSpec((CONV_WIDTH, tn), lambda j, i: (0, j)),
                  pl.BlockSpec((1, tn), lambda j, i: (0, j))],
        out_specs=pl.BlockSpec((tm, tn), lambda j, i: (i, j)),
        out_shape=jax.ShapeDtypeStruct((n, D_INNER), _BF16),
        scratch_shapes=[pltpu.VMEM((k, tn), _BF16), pltpu.VMEM((k, tn), _BF16),
                        pltpu.VMEM((tn // LANES, CONV_HALO + tm, LANES), _F32),
                        pltpu.VMEM((tn // LANES, tm, LANES), _F32)],
        compiler_params=pltpu.CompilerParams(
            dimension_semantics=("arbitrary", "arbitrary"), vmem_limit_bytes=VMEM_LIMIT),
        name="conv_in_glu",
    )(x, w, w, conv_w, conv_b)


def _out_proj_tail(y_scr, h_ref, wout_ref, plg_ref, plb_ref, of_ref, ob_ref):
    acc = jnp.dot(y_scr[...], wout_ref[...], preferred_element_type=_F32)
    o = _layer_norm(ALPHA * h_ref[...] + acc, plg_ref[...], plb_ref[...])
    of_ref[...] = o
    ob_ref[...] = o.astype(_BF16)


def _const_spec(shape):
    return pl.BlockSpec(shape, lambda i: (0,) * len(shape), pipeline_mode=pl.Buffered(1))


def _out_call(kernel, name, tile_inputs, const_inputs, scratch_shapes):
    tm = OUT_TM
    in_specs = [pl.BlockSpec(bs, im) for _, bs, im in tile_inputs]
    in_specs += [_const_spec(a.shape) for a in const_inputs]
    out_spec = pl.BlockSpec((tm, D_MODEL), lambda i: (i, 0))
    return pl.pallas_call(
        kernel,
        grid=(N_TOKENS // tm,),
        in_specs=in_specs,
        out_specs=[out_spec, out_spec],
        out_shape=[jax.ShapeDtypeStruct((N_TOKENS, D_MODEL), _F32),
                   jax.ShapeDtypeStruct((N_TOKENS, D_MODEL), _BF16)],
        scratch_shapes=scratch_shapes,
        compiler_params=pltpu.CompilerParams(
            dimension_semantics=("arbitrary",), vmem_limit_bytes=VMEM_LIMIT),
        name=name,
    )(*[a for a, _, _ in tile_inputs], *const_inputs)


def _halo_spec(rows):
    per_tile = OUT_TM // rows
    return (rows, D_INNER), (lambda i: (jnp.maximum(i * per_tile - 1, 0), 0))


def _sgu_out_kernel(guz_ref, gv_ref, h_ref, lng_ref, lnb_ref, ws_ref, bs_ref,
                    wout_ref, plg_ref, plb_ref, of_ref, ob_ref, vn_scr, y_scr):
    tm = guz_ref.shape[0]
    vn_scr[...] = _layer_norm(gv_ref[...].astype(_F32), lng_ref[...], lnb_ref[...]).astype(_BF16)
    row = lax.broadcasted_iota(jnp.int32, (CHUNK, CHUNK), 0)
    col = lax.broadcasted_iota(jnp.int32, (CHUNK, CHUNK), 1)
    causal = row >= col
    acc = None
    for h0 in range(0, SGU_HEADS, SGU_HEADS_PER_DOT):
        for hd in range(h0, h0 + SGU_HEADS_PER_DOT):
            w = jnp.where(causal, ws_ref[hd], 0.0).astype(_BF16)
            bias = bs_ref[:, hd:hd + 1]
            cs = slice(hd * SGU_HEAD_DIM, (hd + 1) * SGU_HEAD_DIM)
            for c in range(tm // CHUNK):
                rs = slice(c * CHUNK, (c + 1) * CHUNK)
                mixed = jnp.dot(w, vn_scr[rs, cs], preferred_element_type=_F32) + bias
                y_scr[rs, cs] = (guz_ref[rs, cs].astype(_F32) * mixed).astype(_BF16)
        ks = slice(h0 * SGU_HEAD_DIM, (h0 + SGU_HEADS_PER_DOT) * SGU_HEAD_DIM)
        part = jnp.dot(y_scr[:, ks], wout_ref[ks, :], preferred_element_type=_F32)
        acc = part if acc is None else acc + part
    o = _layer_norm(ALPHA * h_ref[...] + acc, plg_ref[...], plb_ref[...])
    of_ref[...] = o
    ob_ref[...] = o.astype(_BF16)


def _sgu_out(guz, gv, h_f, ln_g, ln_b, w_s, b_s_t, w_out, pl_g, pl_b):
    tm = OUT_TM
    tile = (tm, D_INNER)
    return _out_call(
        _sgu_out_kernel, "sgu_out",
        [(guz, tile, lambda i: (i, 0)), (gv, tile, lambda i: (i, 0)),
         (h_f, (tm, D_MODEL), lambda i: (i, 0))],
        [ln_g, ln_b, w_s, b_s_t, w_out, pl_g, pl_b],
        [pltpu.VMEM(tile, _BF16), pltpu.VMEM(tile, _BF16)])


def _pool_out_kernel(v_ref, vh_ref, sz_ref, h_ref, wpool_ref, scale_ref,
                     wout_ref, plg_ref, plb_ref, of_ref, ob_ref, y_scr):
    tm = v_ref.shape[0]
    pos0 = (pl.program_id(0) * tm) % SEQ
    seq_start = pos0 == 0
    gd = POOL_GROUP_DIM
    pos = pos0 + lax.broadcasted_iota(jnp.int32, (tm, gd), 0)
    for g, win in enumerate(POOL_WINDOWS):
        cs = slice(g * gd, (g + 1) * gd)
        halo = jnp.where(seq_start, 0.0, vh_ref[:, cs].astype(_F32))
        ext = jnp.concatenate([halo, v_ref[:, cs].astype(_F32)], axis=0)
        s, k = ext, 1
        while k < win:
            s = s + pltpu.roll(s, k, axis=0)
            k *= 2
        cnt = jnp.minimum(pos + 1, win).astype(_F32)
        p = s[POOL_HALO:] / cnt - ext[POOL_HALO:]
        q = jnp.dot(p.astype(_BF16), wpool_ref[g], preferred_element_type=_F32) * scale_ref[:, cs]
        y_scr[:, cs] = (q * sz_ref[:, cs].astype(_F32)).astype(_BF16)
    _out_proj_tail(y_scr, h_ref, wout_ref, plg_ref, plb_ref, of_ref, ob_ref)


def _pool_out(v, sz, h_f, w_pool, scale, w_out, pl_g, pl_b):
    tm = OUT_TM
    tile = (tm, D_INNER)
    halo_shape, halo_map = _halo_spec(POOL_HALO)
    return _out_call(
        _pool_out_kernel, "pool_out",
        [(v, tile, lambda i: (i, 0)), (v, halo_shape, halo_map), (sz, tile, lambda i: (i, 0)),
         (h_f, (tm, D_MODEL), lambda i: (i, 0))],
        [w_pool, scale, w_out, pl_g, pl_b],
        [pltpu.VMEM(tile, _BF16)])


def _conv_out_kernel(c_ref, sz_ref, h_ref, lng_ref, lnb_ref,
                     wout_ref, plg_ref, plb_ref, of_ref, ob_ref):
    c = c_ref[...].astype(_F32)
    mu = jnp.mean(c, axis=-1, keepdims=True)
    var = jnp.mean(jnp.square(c - mu), axis=-1, keepdims=True)
    rstd = lax.rsqrt(var + LN_EPS)
    acc = None
    for k in range(D_INNER // OUT_K_CHUNK):
        cs = slice(k * OUT_K_CHUNK, (k + 1) * OUT_K_CHUNK)
        s = _silu((c_ref[:, cs].astype(_F32) - mu) * rstd * lng_ref[:, cs] + lnb_ref[:, cs])
        y = (s * sz_ref[:, cs].astype(_F32)).astype(_BF16)
        part = jnp.dot(y, wout_ref[cs, :], preferred_element_type=_F32)
        acc = part if acc is None else acc + part
    o = _layer_norm(ALPHA * h_ref[...] + acc, plg_ref[...], plb_ref[...])
    of_ref[...] = o
    ob_ref[...] = o.astype(_BF16)


def _conv_out(c, sz, h_f, ln_g, ln_b, w_out, pl_g, pl_b):
    tm = OUT_TM
    tile = (tm, D_INNER)
    return _out_call(
        _conv_out_kernel, "conv_out",
        [(c, tile, lambda i: (i, 0)), (sz, tile, lambda i: (i, 0)), (h_f, (tm, D_MODEL), lambda i: (i, 0))],
        [ln_g, ln_b, w_out, pl_g, pl_b],
        [])


def _row(v):
    return v.reshape(1, -1).astype(_F32)


def kernel(x, a_w_in, a_ln_g, a_ln_b, a_w_s, a_b_s, a_w_out, b_w_in, b_w_pool, b_scale, b_w_out,
           c_w_in, c_conv_w, c_conv_b, c_ln_g, c_ln_b, c_w_out, post_ln_g, post_ln_b):
    e_tiles = D_INNER // PROJ_TN
    h_f = x.reshape(N_TOKENS, D_MODEL)
    h_b = h_f.astype(_BF16)
    for i in range(DEPTH):
        kind, j = i % N_MIXERS, i // N_MIXERS
        pl_g, pl_b = _row(post_ln_g[i]), _row(post_ln_b[i])
        if kind == 0:
            guz = _proj(h_b, a_w_in, j, [0, 2 * e_tiles], e_tiles,
                        lambda u, z: _gelu(u) * _silu(z), "sgu_in_uz")
            gv, w_out = _proj(h_b, a_w_in, j, [e_tiles], e_tiles, _gelu, "sgu_in_v", cast=(a_w_out, j))
            h_f, h_b = _sgu_out(guz, gv, h_f, _row(a_ln_g[j]), _row(a_ln_b[j]), a_w_s[j],
                                a_b_s[j].T, w_out, pl_g, pl_b)
        elif kind == 1:
            w_pool_rows = b_w_pool.reshape(b_w_pool.shape[0], D_INNER, POOL_GROUP_DIM)
            v, w_pool = _proj(h_b, b_w_in, j, [0], e_tiles, lambda a: a, "pool_in_v",
                              cast=(w_pool_rows, j))
            sz, w_out = _proj(h_b, b_w_in, j, [e_tiles], e_tiles, _silu, "pool_in_z", cast=(b_w_out, j))
            h_f, h_b = _pool_out(v, sz, h_f, w_pool.reshape(b_w_pool.shape[1:]), _row(b_scale[j]),
                                 w_out, pl_g, pl_b)
        else:
            c = _glu_conv(h_b, c_w_in, j, c_conv_w[j], _row(c_conv_b[j]))
            sz, w_out = _proj(h_b, c_w_in, j, [2 * e_tiles], e_tiles, _silu, "conv_in_z", cast=(c_w_out, j))
            h_f, h_b = _conv_out(c, sz, h_f, _row(c_ln_g[j]), _row(c_ln_b[j]), w_out, pl_g, pl_b)
    return h_f.reshape(BATCH, SEQ, D_MODEL)
```
